```python
import math
import jax, jax.numpy as jnp
from jax import lax
import numpy as np

D_MODEL = 4096
BATCH = 2
SEQ = 4096
DEPTH = 1

CHUNK = 64
Q_BLOCK = 128
RET_HEADS = 8
RET_QK_DIM = 256
RET_V_DIM = 512
RET_THETA = 10000.0
DIFF_HEADS = 16
DIFF_QK_DIM = 128
DIFF_V_DIM = 256
ROPE_THETA = 10000.0
RET_QW = RET_HEADS * RET_QK_DIM
RET_VW = RET_HEADS * RET_V_DIM
DIFF_QW = DIFF_HEADS * 2 * DIFF_QK_DIM
DIFF_VW = DIFF_HEADS * DIFF_V_DIM
IN_SPLITS = (RET_QW, RET_QW, RET_VW, RET_VW, DIFF_QW, DIFF_QW, DIFF_VW, D_MODEL, D_MODEL)
IN_WIDTH = sum(IN_SPLITS)
N_EXPERTS = 64
TOP_K = 8
N_GROUPS = 8
TOPK_GROUPS = 4
EXPERT_DIM = 512
SHARED_DIM = 512
ROUTED_SCALE = 2.5
MOE_BLOCK = 128
DN_ALPHA = (2.0 * DEPTH) ** 0.25
DN_BETA = (8.0 * DEPTH) ** -0.25
EPS = 1e-5

kernel_name = "hybrid_retention_diffattn_moe_block"


def layernorm(x, w, b):
    xf = x.astype(jnp.float32)
    mu = xf.mean(-1, keepdims=True)
    var = jnp.square(xf - mu).mean(-1, keepdims=True)
    return ((xf - mu) * lax.rsqrt(var + EPS) * w + b).astype(x.dtype)


def head_layernorm(x):
    xf = x.astype(jnp.float32)
    mu = xf.mean(-1, keepdims=True)
    var = jnp.square(xf - mu).mean(-1, keepdims=True)
    return (xf - mu) * lax.rsqrt(var + EPS)


def head_rmsnorm(x, w):
    xf = x.astype(jnp.float32)
    return xf * lax.rsqrt(jnp.square(xf).mean(-1, keepdims=True) + EPS) * w


def rotate(x, positions, freqs):
    ang = positions.astype(jnp.float32)[..., None] * freqs
    cos = jnp.cos(ang)[:, :, None, :].astype(x.dtype)
    sin = jnp.sin(ang)[:, :, None, :].astype(x.dtype)
    half = x.shape[-1] // 2
    x1, x2 = x[..., :half], x[..., half:]
    return jnp.concatenate([x1 * cos - x2 * sin, x2 * cos + x1 * sin], axis=-1)


def retention_chunkwise(q, k, v):
    B, S, H, dk = q.shape
    dv = v.shape[-1]
    nc = S // CHUNK
    log_g = jnp.log1p(-jnp.exp2(-5.0 - jnp.arange(H, dtype=jnp.float32)))
    idx = jnp.arange(CHUNK, dtype=jnp.float32)
    inner_decay = jnp.exp(log_g[:, None, None] * jnp.abs(idx[:, None] - idx[None, :]))
    q_decay = jnp.exp(log_g[:, None] * (idx + 1.0))[..., None]
    k_decay = jnp.exp(log_g[:, None] * (CHUNK - 1.0 - idx))[..., None]
    chunk_decay = jnp.exp(log_g * CHUNK)[:, None, None]

    def to_chunks(t):
        return t.reshape(B, nc, CHUNK, H, t.shape[-1]).transpose(1, 0, 3, 2, 4)

    def step(R, inp):
        qc, kc, vc = inp
        s = jnp.einsum('bhid,bhjd->bhij', qc, kc) * inner_decay
        inner = jnp.einsum('bhij,bhjv->bhiv', s, vc)
        cross = jnp.einsum('bhid,bhdv->bhiv', qc * q_decay, R)
        R_new = R * chunk_decay + jnp.einsum('bhjd,bhjv->bhdv', kc * k_decay, vc)
        return R_new, (inner + cross).astype(jnp.float32)

    R0 = jnp.zeros((B, H, dk, dv), jnp.float32)
    _, ys = lax.scan(step, R0, (to_chunks(q), to_chunks(k), to_chunks(v)))
    return ys.transpose(1, 0, 3, 2, 4).reshape(B, S, H, dv)


def diff_attention(q, k, v, lam):
    B, S, H2, dq = q.shape
    H = H2 // 2
    nb = S // Q_BLOCK
    scale = DIFF_QK_DIM ** -0.5
    q_blocks = q.reshape(B, nb, Q_BLOCK, H2, dq).transpose(1, 0, 3, 2, 4)
    k_t = k.transpose(0, 2, 1, 3)
    key_chunk = jnp.arange(S) // CHUNK

    def attend(args):
        q_blk, bi = args
        q_chunk = (bi * Q_BLOCK + jnp.arange(Q_BLOCK)) // CHUNK
        mask = key_chunk[None, :] <= q_chunk[:, None]
        s = jnp.einsum('bhqd,bhkd->bhqk', q_blk, k_t).astype(jnp.float32) * scale
        s = jnp.where(mask[None, None], s, -jnp.inf)
        p = jax.nn.softmax(s, axis=-1).reshape(B, H, 2, Q_BLOCK, S)
        a = p[:, :, 0] - lam * p[:, :, 1]
        return jnp.einsum('bhqk,bkhv->bqhv', a.astype(v.dtype), v)

    out = lax.map(attend, (q_blocks, jnp.arange(nb)))
    return out.transpose(1, 0, 2, 3, 4).reshape(B, S, H, v.shape[-1])


def token_mixer(x, positions, w_in, w_ret_proj, w_diff_proj, w_out,
                lambda_q1, lambda_k1, lambda_q2, lambda_k2, diff_norm_w, lambda_init):
    B, S, _ = x.shape
    proj = jnp.einsum('bsd,de->bse', x, w_in)
    split_points = [int(p) for p in np.cumsum(IN_SPLITS)[:-1]]
    rq, rk, rv, rg, dq, dk, dv, gate_ret, gate_diff = jnp.split(proj, split_points, axis=-1)

    ret_freqs = 1.0 / (RET_THETA ** jnp.linspace(0.0, 1.0, RET_QK_DIM // 2, dtype=jnp.float32))
    rq = rotate(rq.reshape(B, S, RET_HEADS, RET_QK_DIM), positions, ret_freqs)
    rk = rotate(rk.reshape(B, S, RET_HEADS, RET_QK_DIM), positions, ret_freqs) * (RET_QK_DIM ** -0.5)
    rv = rv.reshape(B, S, RET_HEADS, RET_V_DIM)
    ret = retention_chunkwise(rq, rk, rv)
    ret = head_layernorm(ret).astype(x.dtype).reshape(B, S, RET_VW) * jax.nn.silu(rg)

    rope_freqs = ROPE_THETA ** (-jnp.arange(0, DIFF_QK_DIM, 2, dtype=jnp.float32) / DIFF_QK_DIM)
    dq = rotate(dq.reshape(B, S, 2 * DIFF_HEADS, DIFF_QK_DIM), positions, rope_freqs)
    dk = rotate(dk.reshape(B, S, 2 * DIFF_HEADS, DIFF_QK_DIM), positions, rope_freqs)
    dv = dv.reshape(B, S, DIFF_HEADS, DIFF_V_DIM)
    lam = (jnp.exp(jnp.sum(lambda_q1.astype(jnp.float32) * lambda_k1.astype(jnp.float32)))
           - jnp.exp(jnp.sum(lambda_q2.astype(jnp.float32) * lambda_k2.astype(jnp.float32)))
           + lambda_init)
    da = diff_attention(dq, dk, dv, lam)
    da = (head_rmsnorm(da, diff_norm_w.astype(jnp.float32)) * (1.0 - lambda_init)).astype(x.dtype)
    da = da.reshape(B, S, DIFF_VW)

    merged = (jax.nn.sigmoid(gate_ret) * jnp.einsum('bsv,vd->bsd', ret, w_ret_proj)
              + jax.nn.sigmoid(gate_diff) * jnp.einsum('bsv,vd->bsd', da, w_diff_proj))
    return jnp.einsum('bsd,de->bse', merged, w_out)


def moe(h, w_router, router_bias, exp_gate, exp_up, exp_down, shared_gate, shared_up, shared_down):
    B, S, D = h.shape
    T = B * S
    h2 = h.reshape(T, D)
    scores = jax.nn.sigmoid(jnp.einsum('td,de->te', h2, w_router).astype(jnp.float32))
    biased = scores + router_bias.astype(jnp.float32)
    per_group = N_EXPERTS // N_GROUPS
    group_score = lax.top_k(biased.reshape(T, N_GROUPS, per_group), 2)[0].sum(-1)
    _, top_groups = lax.top_k(group_score, TOPK_GROUPS)
    group_mask = jax.nn.one_hot(top_groups, N_GROUPS, dtype=jnp.float32).sum(1)
    expert_mask = jnp.repeat(group_mask, per_group, axis=1) > 0
    _, top_e = lax.top_k(jnp.where(expert_mask, biased, -jnp.inf), TOP_K)
    w_sel = jnp.take_along_axis(scores, top_e, axis=-1)
    w_sel = w_sel / w_sel.sum(-1, keepdims=True) * ROUTED_SCALE

    A = T * TOP_K
    flat_e = top_e.reshape(A)
    flat_tok = jnp.arange(A, dtype=jnp.int32) // TOP_K
    flat_w = w_sel.reshape(A)
    order = jnp.argsort(flat_e)
    sorted_e = flat_e[order]
    counts = jnp.bincount(flat_e, length=N_EXPERTS)
    padded = (counts + MOE_BLOCK - 1) // MOE_BLOCK * MOE_BLOCK
    pad_end = jnp.cumsum(padded)
    pad_start = pad_end - padded
    start = jnp.cumsum(counts) - counts
    dest = pad_start[sorted_e] + jnp.arange(A) - start[sorted_e]
    nb = A // MOE_BLOCK + N_EXPERTS
    P = nb * MOE_BLOCK
    slot_tok = jnp.zeros((P,), jnp.int32).at[dest].set(flat_tok[order])
    slot_w = jnp.zeros((P,), jnp.float32).at[dest].set(flat_w[order])
    block_e = jnp.minimum(jnp.searchsorted(pad_end, jnp.arange(nb) * MOE_BLOCK, side='right'),
                          N_EXPERTS - 1)

    def expert_block(acc, inp):
        tok, wt, e = inp
        xb = h2[tok]
        y = (jax.nn.silu(xb @ exp_gate[e]) * (xb @ exp_up[e])) @ exp_down[e]
        return acc.at[tok].add(y * wt[:, None]), None

    routed, _ = lax.scan(expert_block, jnp.zeros((T, D), jnp.float32),
                         (slot_tok.reshape(nb, MOE_BLOCK), slot_w.reshape(nb, MOE_BLOCK), block_e))
    shared = (jax.nn.silu(h2 @ shared_gate) * (h2 @ shared_up)) @ shared_down
    return (shared + routed).astype(h.dtype).reshape(B, S, D)


def setup_inputs(seed: int = 0) -> dict:
    key = jax.random.key(seed)
    ks = jax.random.split(key, 24)
    f32 = jnp.float32
    nrm = lambda k, shape, s: jax.random.normal(k, shape, f32) * s
    col_scale = jnp.concatenate([
        jnp.ones((2 * RET_QW,), f32), jnp.full((RET_VW,), DN_BETA, f32), jnp.ones((RET_VW,), f32),
        jnp.ones((2 * DIFF_QW,), f32), jnp.full((DIFF_VW,), DN_BETA, f32), jnp.ones((2 * D_MODEL,), f32)])
    offset = jax.random.randint(ks[1], (BATCH, 1), 0, 4096, dtype=jnp.int32)
    return {
        "x": nrm(ks[0], (BATCH, SEQ, D_MODEL), 1.0),
        "positions": offset + jnp.arange(SEQ, dtype=jnp.int32)[None, :],
        "w_in": nrm(ks[2], (DEPTH, D_MODEL, IN_WIDTH), D_MODEL ** -0.5) * col_scale,
        "w_ret_proj": nrm(ks[3], (DEPTH, RET_VW, D_MODEL), DN_BETA * RET_VW ** -0.5),
        "w_diff_proj": nrm(ks[4], (DEPTH, DIFF_VW, D_MODEL), DN_BETA * DIFF_VW ** -0.5),
        "w_out": nrm(ks[5], (DEPTH, D_MODEL, D_MODEL), DN_BETA * D_MODEL ** -0.5),
        "lambda_q1": nrm(ks[6], (DEPTH, DIFF_QK_DIM), 0.1),
        "lambda_k1": nrm(ks[7], (DEPTH, DIFF_QK_DIM), 0.1),
        "lambda_q2": nrm(ks[8], (DEPTH, DIFF_QK_DIM), 0.1),
        "lambda_k2": nrm(ks[9], (DEPTH, DIFF_QK_DIM), 0.1),
        "diff_norm_w": 1.0 + nrm(ks[10], (DEPTH, DIFF_V_DIM), 0.02),
        "ln1_w": 1.0 + nrm(ks[11], (DEPTH, D_MODEL), 0.02),
        "ln1_b": nrm(ks[12], (DEPTH, D_MODEL), 0.02),
        "w_router": nrm(ks[13], (DEPTH, D_MODEL, N_EXPERTS), D_MODEL ** -0.5),
        "router_bias": nrm(ks[14], (DEPTH, N_EXPERTS), 0.01),
        "exp_gate": nrm(ks[15], (DEPTH, N_EXPERTS, D_MODEL, EXPERT_DIM), DN_BETA * D_MODEL ** -0.5),
        "exp_up": nrm(ks[16], (DEPTH, N_EXPERTS, D_MODEL, EXPERT_DIM), DN_BETA * D_MODEL ** -0.5),
        "exp_down": nrm(ks[17], (DEPTH, N_EXPERTS, EXPERT_DIM, D_MODEL), DN_BETA * EXPERT_DIM ** -0.5),
        "shared_gate": nrm(ks[18], (DEPTH, D_MODEL, SHARED_DIM), DN_BETA * D_MODEL ** -0.5),
        "shared_up": nrm(ks[19], (DEPTH, D_MODEL, SHARED_DIM), DN_BETA * D_MODEL ** -0.5),
        "shared_down": nrm(ks[20], (DEPTH, SHARED_DIM, D_MODEL), DN_BETA * SHARED_DIM ** -0.5),
        "ln2_w": 1.0 + nrm(ks[21], (DEPTH, D_MODEL), 0.02),
        "ln2_b": nrm(ks[22], (DEPTH, D_MODEL), 0.02),
    }


def reference(x, positions, w_in, w_ret_proj, w_diff_proj, w_out,
              lambda_q1, lambda_k1, lambda_q2, lambda_k2, diff_norm_w, ln1_w, ln1_b,
              w_router, router_bias, exp_gate, exp_up, exp_down,
              shared_gate, shared_up, shared_down, ln2_w, ln2_b):
    h = x
    for l in range(DEPTH):
        lambda_init = 0.8 - 0.6 * math.exp(-0.3 * l)
        mix = token_mixer(h, positions, w_in[l], w_ret_proj[l], w_diff_proj[l], w_out[l],
                          lambda_q1[l], lambda_k1[l], lambda_q2[l], lambda_k2[l], diff_norm_w[l],
                          lambda_init)
        h = layernorm(DN_ALPHA * h + mix, ln1_w[l], ln1_b[l])
        ffn = moe(h, w_router[l], router_bias[l], exp_gate[l], exp_up[l], exp_down[l],
                  shared_gate[l], shared_up[l], shared_down[l])
        h = layernorm(DN_ALPHA * h + ffn, ln2_w[l], ln2_b[l])
    return h
```

```python
import functools
import math

import jax
import jax.numpy as jnp
from jax import lax
from jax.experimental import pallas as pl
from jax.experimental.pallas import tpu as pltpu

F32 = jnp.float32
BF16 = jnp.bfloat16

D_MODEL = 4096
CHUNK = 64
RET_HEADS = 8
RET_QK_DIM = 256
RET_V_DIM = 512
RET_THETA = 10000.0
DIFF_HEADS = 16
DIFF_QK_DIM = 128
DIFF_V_DIM = 256
ROPE_THETA = 10000.0
RET_QW = RET_HEADS * RET_QK_DIM
RET_VW = RET_HEADS * RET_V_DIM
DIFF_QW = DIFF_HEADS * 2 * DIFF_QK_DIM
DIFF_VW = DIFF_HEADS * DIFF_V_DIM
N_EXPERTS = 64
TOP_K = 8
N_GROUPS = 8
TOPK_GROUPS = 4
EXPERT_DIM = 512
ROUTED_SCALE = 2.5
DEPTH = 1
DN_ALPHA = (2.0 * DEPTH) ** 0.25
EPS = 1e-5

LANES = 128
MIB = 1024 * 1024
VMEM_LIMIT = 56 * MIB

PROJ_TM, PROJ_TN = 1024, 512
MERGE_TM, MERGE_TN = 512, 512
RET_BLOCK = 256
ATT_BLOCK = 512
LN_TM = 256
MOE_BLOCK = 256
SHARED_TM = 256
FINAL_TM = 128


def _params(n_axes):
    return pltpu.CompilerParams(dimension_semantics=("arbitrary",) * n_axes,
                                vmem_limit_bytes=VMEM_LIMIT)


def _sigmoid(v):
    return 1.0 / (1.0 + jnp.exp(-v))


def _proj_kernel(*refs, mode, scale, tn):
    if mode in ("ret_rot", "diff_rot"):
        x_ref, w_ref, c_ref, s_ref, o_ref = refs
    else:
        x_ref, w_ref, o_ref = refs
    acc = jnp.dot(x_ref[...], w_ref[...], preferred_element_type=F32)
    if mode == "ret_rot":
        cos, sin = c_ref[...], s_ref[...]
        for h in range(tn // RET_QK_DIM):
            lo = h * RET_QK_DIM
            x1 = acc[:, lo:lo + LANES]
            x2 = acc[:, lo + LANES:lo + 2 * LANES]
            o_ref[:, lo:lo + LANES] = ((x1 * cos - x2 * sin) * scale).astype(o_ref.dtype)
            o_ref[:, lo + LANES:lo + 2 * LANES] = ((x2 * cos + x1 * sin) * scale).astype(o_ref.dtype)
    elif mode == "diff_rot":
        c, s = c_ref[...], s_ref[...]
        for g in range(tn // LANES):
            xg = acc[:, g * LANES:(g + 1) * LANES]
            rot = pltpu.roll(xg, LANES // 2, axis=1)
            o_ref[:, g * LANES:(g + 1) * LANES] = ((xg * c + rot * s) * scale).astype(o_ref.dtype)
    elif mode == "silu":
        o_ref[...] = (acc * _sigmoid(acc)).astype(o_ref.dtype)
    elif mode == "sigmoid":
        o_ref[...] = _sigmoid(acc).astype(o_ref.dtype)
    else:
        o_ref[...] = acc.astype(o_ref.dtype)


def _proj(xb, w, col_off, width, mode, scale=1.0, rot=None):
    t, d = xb.shape
    tm, tn = min(PROJ_TM, t), PROJ_TN
    joff = col_off // tn
    in_specs = [pl.BlockSpec((tm, d), lambda i, j: (i, 0)),
                pl.BlockSpec((d, tn), lambda i, j: (0, j + joff))]
    args = [xb, w]
    if rot is not None:
        in_specs += [pl.BlockSpec((tm, LANES), lambda i, j: (i, 0))] * 2
        args += list(rot)
    return pl.pallas_call(
        functools.partial(_proj_kernel, mode=mode, scale=scale, tn=tn),
        grid=(t // tm, width // tn),
        in_specs=in_specs,
        out_specs=pl.BlockSpec((tm, tn), lambda i, j: (i, j)),
        out_shape=jax.ShapeDtypeStruct((t, width), BF16),
        compiler_params=_params(2),
        name="proj_" + mode,
    )(*args)


def _ret_kernel(q_ref, k_ref, v_ref, g_ref, dm_ref, qd_ref, kd_ref, cd_ref, o_ref, r_ref):
    @pl.when(pl.program_id(2) == 0)
    def _():
        r_ref[...] = jnp.zeros_like(r_ref)

    q, k, v = q_ref[...], k_ref[...], v_ref[...]
    s = lax.dot_general(q, k, (((1,), (1,)), ((), ())), preferred_element_type=F32)
    s = (s * dm_ref[0]).astype(BF16)
    inner = jnp.dot(s, v, preferred_element_type=F32)
    qs = (q.astype(F32) * qd_ref[0]).astype(BF16)
    r = r_ref[...]
    cross = jnp.dot(qs, r.astype(BF16), preferred_element_type=F32)
    ks = (k.astype(F32) * kd_ref[0]).astype(BF16)
    r_ref[...] = r * cd_ref[0] + lax.dot_general(
        ks, v, (((0,), (0,)), ((), ())), preferred_element_type=F32)
    y = inner + cross
    mu = jnp.mean(y, axis=-1, keepdims=True)
    yc = y - mu
    var = jnp.mean(yc * yc, axis=-1, keepdims=True)
    o_ref[...] = (yc * lax.rsqrt(var + EPS) * g_ref[...].astype(F32)).astype(o_ref.dtype)


def _retention_tables(blk):
    log_g = jnp.log1p(-jnp.exp2(-5.0 - jnp.arange(RET_HEADS, dtype=F32)))
    idx = jnp.arange(blk, dtype=F32)
    chunk = jnp.arange(blk) // CHUNK
    visible = chunk[None, :] <= chunk[:, None]
    dm = jnp.where(visible[None],
                   jnp.exp(log_g[:, None, None] * jnp.abs(idx[:, None] - idx[None, :])), 0.0)
    qd = jnp.exp(log_g[:, None] * (idx + 1.0))
    kd = jnp.exp(log_g[:, None] * (blk - 1.0 - idx))
    cd = jnp.exp(log_g * blk)
    qd = jnp.broadcast_to(qd[:, :, None], (RET_HEADS, blk, RET_QK_DIM))
    kd = jnp.broadcast_to(kd[:, :, None], (RET_HEADS, blk, RET_QK_DIM))
    cd = jnp.broadcast_to(cd[:, None, None], (RET_HEADS, 1, RET_V_DIM))
    return dm, qd, kd, cd


def _retention(rq, rk, rv, rg, batch, seq):
    t = batch * seq
    blk = min(RET_BLOCK, seq)
    nl = seq // blk
    dm, qd, kd, cd = _retention_tables(blk)
    row = lambda b, h, l: (b * nl + l, h)
    head = lambda b, h, l: (h, 0, 0)
    return pl.pallas_call(
        _ret_kernel,
        grid=(batch, RET_HEADS, nl),
        in_specs=[pl.BlockSpec((blk, RET_QK_DIM), row),
                  pl.BlockSpec((blk, RET_QK_DIM), row),
                  pl.BlockSpec((blk, RET_V_DIM), row),
                  pl.BlockSpec((blk, RET_V_DIM), row),
                  pl.BlockSpec((1, blk, blk), head),
                  pl.BlockSpec((1, blk, RET_QK_DIM), head),
                  pl.BlockSpec((1, blk, RET_QK_DIM), head),
                  pl.BlockSpec((1, 1, RET_V_DIM), head)],
        out_specs=pl.BlockSpec((blk, RET_V_DIM), row),
        out_shape=jax.ShapeDtypeStruct((t, RET_VW), BF16),
        scratch_shapes=[pltpu.VMEM((RET_QK_DIM, RET_V_DIM), F32)],
        compiler_params=_params(3),
        name="retention",
    )(rq, rk, rv, rg, dm, qd, kd, cd)


def _attn_kernel(lq1_ref, lk1_ref, lq2_ref, lk2_ref, nw_ref, q_ref, k_ref, v_ref, o_ref,
                 m_ref, l_ref, acc_ref, *, blk, lambda_init):
    qi = pl.program_id(2)
    m_ref[...] = jnp.full_like(m_ref, -jnp.inf)
    l_ref[...] = jnp.zeros_like(l_ref)
    acc_ref[...] = jnp.zeros_like(acc_ref)
    q = q_ref[...]

    def block(kj, masked):
        off = pl.multiple_of(kj * blk, blk)
        kb = k_ref[pl.ds(off, blk), :]
        vb = v_ref[pl.ds(off, blk), :]
        if masked:
            rchunk = lax.broadcasted_iota(jnp.int32, (blk, blk), 0) // CHUNK
            cchunk = lax.broadcasted_iota(jnp.int32, (blk, blk), 1) // CHUNK
            visible = cchunk <= rchunk
        for c in range(2):
            qc = q[:, c * DIFF_QK_DIM:(c + 1) * DIFF_QK_DIM]
            kc = kb[:, c * DIFF_QK_DIM:(c + 1) * DIFF_QK_DIM]
            s = lax.dot_general(qc, kc, (((1,), (1,)), ((), ())), preferred_element_type=F32)
            if masked:
                s = jnp.where(visible, s, -jnp.inf)
            m_prev = m_ref[c]
            m_new = jnp.maximum(m_prev, jnp.max(s, axis=-1, keepdims=True))
            p = jnp.exp(s - m_new)
            alpha = jnp.exp(m_prev - m_new)
            l_ref[c] = alpha * l_ref[c] + jnp.sum(p, axis=-1, keepdims=True)
            acc_ref[c] = alpha * acc_ref[c] + jnp.dot(p.astype(BF16), vb,
                                                      preferred_element_type=F32)
            m_ref[c] = m_new

    def body(kj, carry):
        block(kj, False)
        return carry

    lax.fori_loop(0, qi, body, 0)
    block(qi, True)

    lam = (jnp.exp(jnp.sum(lq1_ref[...] * lk1_ref[...], axis=-1, keepdims=True))
           - jnp.exp(jnp.sum(lq2_ref[...] * lk2_ref[...], axis=-1, keepdims=True))
           + lambda_init)
    o = acc_ref[0] / l_ref[0] - lam * (acc_ref[1] / l_ref[1])
    ms = jnp.mean(o * o, axis=-1, keepdims=True)
    o_ref[...] = (o * lax.rsqrt(ms + EPS) * nw_ref[...] * (1.0 - lambda_init)).astype(o_ref.dtype)


def _diff_attention(dq, dk, dv, lq1, lk1, lq2, lk2, norm_w, batch, seq, lambda_init):
    t = batch * seq
    blk = min(ATT_BLOCK, seq)
    nq = seq // blk
    hw = 2 * DIFF_QK_DIM
    vec = lambda b, h, i: (0, 0)
    return pl.pallas_call(
        functools.partial(_attn_kernel, blk=blk, lambda_init=lambda_init),
        grid=(batch, DIFF_HEADS, nq),
        in_specs=[pl.BlockSpec((1, DIFF_QK_DIM), vec)] * 4
        + [pl.BlockSpec((1, DIFF_V_DIM), vec),
           pl.BlockSpec((blk, hw), lambda b, h, i: (b * nq + i, h)),
           pl.BlockSpec((seq, hw), lambda b, h, i: (b, h)),
           pl.BlockSpec((seq, DIFF_V_DIM), lambda b, h, i: (b, h))],
        out_specs=pl.BlockSpec((blk, DIFF_V_DIM), lambda b, h, i: (b * nq + i, h)),
        out_shape=jax.ShapeDtypeStruct((t, DIFF_VW), BF16),
        scratch_shapes=[pltpu.VMEM((2, blk, 1), F32),
                        pltpu.VMEM((2, blk, 1), F32),
                        pltpu.VMEM((2, blk, DIFF_V_DIM), F32)],
        compiler_params=_params(3),
        name="diff_attention",
    )(lq1, lk1, lq2, lk2, norm_w, dq, dk, dv)


def _merge_kernel(r_ref, d_ref, wr_ref, wd_ref, gr_ref, gd_ref, o_ref):
    a = jnp.dot(r_ref[...], wr_ref[...], preferred_element_type=F32)
    b = jnp.dot(d_ref[...], wd_ref[...], preferred_element_type=F32)
    o_ref[...] = (gr_ref[...].astype(F32) * a + gd_ref[...].astype(F32) * b).astype(o_ref.dtype)


def _merge(ret, da, wr, wd, gr, gd):
    t, kdim = ret.shape
    n = wr.shape[1]
    tm, tn = min(MERGE_TM, t), MERGE_TN
    lhs = pl.BlockSpec((tm, kdim), lambda i, j: (i, 0))
    rhs = pl.BlockSpec((kdim, tn), lambda i, j: (0, j))
    tile = pl.BlockSpec((tm, tn), lambda i, j: (i, j))
    return pl.pallas_call(
        _merge_kernel,
        grid=(t // tm, n // tn),
        in_specs=[lhs, lhs, rhs, rhs, tile, tile],
        out_specs=tile,
        out_shape=jax.ShapeDtypeStruct((t, n), BF16),
        compiler_params=_params(2),
        name="merge",
    )(ret, da, wr, wd, gr, gd)


def _outproj_kernel(m_ref, w_ref, x_ref, o_ref):
    o_ref[...] = DN_ALPHA * x_ref[...] + jnp.dot(m_ref[...], w_ref[...],
                                                  preferred_element_type=F32)


def _outproj(merged, w, x):
    t, kdim = merged.shape
    n = w.shape[1]
    tm, tn = min(PROJ_TM, t), PROJ_TN
    tile = pl.BlockSpec((tm, tn), lambda i, j: (i, j))
    return pl.pallas_call(
        _outproj_kernel,
        grid=(t // tm, n // tn),
        in_specs=[pl.BlockSpec((tm, kdim), lambda i, j: (i, 0)),
                  pl.BlockSpec((kdim, tn), lambda i, j: (0, j)),
                  tile],
        out_specs=tile,
        out_shape=jax.ShapeDtypeStruct((t, n), F32),
        compiler_params=_params(2),
        name="outproj",
    )(merged, w, x)


def _layernorm(y, w, b):
    mu = jnp.mean(y, axis=-1, keepdims=True)
    yc = y - mu
    var = jnp.mean(yc * yc, axis=-1, keepdims=True)
    return yc * lax.rsqrt(var + EPS) * w + b


def _ln_route_kernel(y_ref, lw_ref, lb_ref, wr_ref, rb_ref, h_ref, te_ref, tw_ref):
    h = _layernorm(y_ref[...], lw_ref[...], lb_ref[...])
    h_ref[...] = h
    hi = h.astype(BF16)
    lo = (h - hi.astype(F32)).astype(BF16)
    w = wr_ref[...]
    whi = w.astype(BF16)
    wlo = (w - whi.astype(F32)).astype(BF16)
    logits = (jnp.dot(hi, whi, preferred_element_type=F32)
              + (jnp.dot(hi, wlo, preferred_element_type=F32)
                 + jnp.dot(lo, whi, preferred_element_type=F32)))
    scores = _sigmoid(logits)
    biased = scores + rb_ref[...]
    tm = scores.shape[0]
    neg = -jnp.inf
    lane_i = lax.broadcasted_iota(jnp.int32, (tm, N_EXPERTS), 1)
    lane = lane_i.astype(F32)
    per_group = N_EXPERTS // N_GROUPS
    grp = lane_i // per_group

    gscores = []
    gs_lane = jnp.zeros((tm, N_EXPERTS), F32)
    for g in range(N_GROUPS):
        mk = grp == g
        vals = jnp.where(mk, biased, neg)
        m1 = jnp.max(vals, axis=-1, keepdims=True)
        i1 = jnp.min(jnp.where(vals == m1, lane, float(N_EXPERTS)), axis=-1, keepdims=True)
        m2 = jnp.max(jnp.where(lane == i1, neg, vals), axis=-1, keepdims=True)
        gscores.append(m1 + m2)
        gs_lane = jnp.where(mk, m1 + m2, gs_lane)
    beaten = jnp.zeros((tm, N_EXPERTS), F32)
    for g in range(N_GROUPS):
        wins = jnp.where(gscores[g] > gs_lane, 1.0,
                         jnp.where(gscores[g] == gs_lane,
                                   jnp.where(grp > g, 1.0, 0.0), 0.0))
        beaten = beaten + wins
    cur = jnp.where(beaten < float(TOPK_GROUPS), biased, neg)

    out_lane = lax.broadcasted_iota(jnp.int32, (tm, LANES), 1)
    sel_e = jnp.zeros((tm, LANES), F32)
    sel_w = jnp.zeros((tm, LANES), F32)
    wsum = jnp.zeros((tm, 1), F32)
    for k in range(TOP_K):
        m = jnp.max(cur, axis=-1, keepdims=True)
        idx = jnp.min(jnp.where(cur == m, lane, float(N_EXPERTS)), axis=-1, keepdims=True)
        hit = lane == idx
        wk = jnp.sum(jnp.where(hit, scores, 0.0), axis=-1, keepdims=True)
        cur = jnp.where(hit, neg, cur)
        wsum = wsum + wk
        sel_e = jnp.where(out_lane == k, idx, sel_e)
        sel_w = jnp.where(out_lane == k, wk, sel_w)
    te_ref[...] = sel_e.astype(jnp.int32)
    tw_ref[...] = sel_w / wsum * ROUTED_SCALE


def _ln_route(y, ln_w, ln_b, w_router, router_bias):
    t, d = y.shape
    tm = min(LN_TM, t)
    rowt = pl.BlockSpec((tm, d), lambda i: (i, 0))
    vec = pl.BlockSpec((1, d), lambda i: (0, 0))
    narrow = pl.BlockSpec((tm, LANES), lambda i: (i, 0))
    return pl.pallas_call(
        _ln_route_kernel,
        grid=(t // tm,),
        in_specs=[rowt, vec, vec,
                  pl.BlockSpec((d, N_EXPERTS), lambda i: (0, 0)),
                  pl.BlockSpec((1, N_EXPERTS), lambda i: (0, 0))],
        out_specs=[rowt, narrow, narrow],
        out_shape=[jax.ShapeDtypeStruct((t, d), F32),
                   jax.ShapeDtypeStruct((t, LANES), jnp.int32),
                   jax.ShapeDtypeStruct((t, LANES), F32)],
        compiler_params=_params(1),
        name="ln_route",
    )(y, ln_w, ln_b, w_router, router_bias)


def _expert_plan(top_e, t):
    a = t * TOP_K
    nb = a // MOE_BLOCK + N_EXPERTS
    p = nb * MOE_BLOCK
    flat_e = top_e.reshape(a)
    order = jnp.argsort(flat_e).astype(jnp.int32)
    sorted_e = flat_e[order]
    counts = jnp.bincount(flat_e, length=N_EXPERTS).astype(jnp.int32)
    padded = (counts + MOE_BLOCK - 1) // MOE_BLOCK * MOE_BLOCK
    pad_end = jnp.cumsum(padded)
    pad_start = pad_end - padded
    start = jnp.cumsum(counts) - counts
    dest = pad_start[sorted_e] + jnp.arange(a, dtype=jnp.int32) - start[sorted_e]
    slot_a = jnp.full((p,), -1, jnp.int32).at[dest].set(order)
    valid = slot_a >= 0
    tok = slot_a // TOP_K
    slot_tok = jnp.where(valid, tok, 0)
    slot_dst = jnp.where(valid, (slot_a % TOP_K) * t + tok, 0)
    n_active = (pad_end[-1] // MOE_BLOCK).astype(jnp.int32)
    blk_start = jnp.arange(nb, dtype=jnp.int32) * MOE_BLOCK
    block_e = jnp.minimum(jnp.searchsorted(pad_end, blk_start, side="right"),
                          N_EXPERTS - 1).astype(jnp.int32)
    n_valid = jnp.clip(counts[block_e] - (blk_start - pad_start[block_e]), 0, MOE_BLOCK)
    active = jnp.arange(nb) < n_active
    n_valid = jnp.where(active, n_valid, 0).astype(jnp.int32)
    last_e = block_e[jnp.maximum(n_active - 1, 0)]
    block_e = jnp.where(active, block_e, last_e)
    return (block_e, n_valid, n_active.reshape(1),
            slot_tok.reshape(nb, 1, MOE_BLOCK), slot_dst.reshape(nb, 1, MOE_BLOCK))


def _expert_kernel(be_ref, nv_ref, na_ref, tok_ref, tokn_ref, dst_ref, h_hbm, wg_ref, wu_ref,
                   wd_ref, yk_hbm, xbuf, ybuf, gsem, ssem):
    i = pl.program_id(0)
    na = na_ref[0]
    slot = lax.rem(i, 2)

    def gather_start(idx_ref, s):
        def body(r, carry):
            pltpu.make_async_copy(h_hbm.at[pl.ds(idx_ref[0, 0, r], 1)],
                                  xbuf.at[s, pl.ds(r, 1)], gsem.at[s]).start()
            return carry
        lax.fori_loop(0, MOE_BLOCK, body, 0)

    def gather_wait(s):
        pltpu.make_async_copy(h_hbm.at[pl.ds(0, MOE_BLOCK)], xbuf.at[s], gsem.at[s]).wait()

    def scatter_start(n):
        def body(r, carry):
            pltpu.make_async_copy(ybuf.at[pl.ds(r, 1)],
                                  yk_hbm.at[pl.ds(dst_ref[0, 0, r], 1)], ssem).start()
            return carry
        lax.fori_loop(0, n, body, 0)

    def scatter_wait(n):
        rows = MOE_BLOCK
        while rows >= 1:
            @pl.when((n & rows) != 0)
            def _():
                pltpu.make_async_copy(ybuf.at[pl.ds(0, rows)], yk_hbm.at[pl.ds(0, rows)],
                                      ssem).wait()
            rows //= 2

    @pl.when(i < na)
    def _():
        @pl.when(i == 0)
        def _():
            gather_start(tok_ref, 0)

        @pl.when(i + 1 < na)
        def _():
            gather_start(tokn_ref, 1 - slot)

        gather_wait(slot)
        x = xbuf[slot].astype(BF16)
        g = jnp.dot(x, wg_ref[...], preferred_element_type=F32)
        u = jnp.dot(x, wu_ref[...], preferred_element_type=F32)
        act = (g * _sigmoid(g) * u).astype(BF16)
        y = jnp.dot(act, wd_ref[...], preferred_element_type=F32)

        @pl.when(i > 0)
        def _():
            scatter_wait(nv_ref[jnp.maximum(i - 1, 0)])

        ybuf[...] = y
        scatter_start(nv_ref[i])

        @pl.when(i == na - 1)
        def _():
            scatter_wait(nv_ref[i])


def _routed_experts(h, plan, wg, wu, wd):
    t, d = h.shape
    block_e, n_valid, n_active, slot_tok, slot_dst = plan
    nb = slot_tok.shape[0]
    e_dim = wg.shape[-1]
    idx_spec = lambda fn: pl.BlockSpec((1, 1, MOE_BLOCK), fn, memory_space=pltpu.SMEM)
    grid_spec = pltpu.PrefetchScalarGridSpec(
        num_scalar_prefetch=3,
        grid=(nb,),
        in_specs=[idx_spec(lambda i, be, nv, na: (i, 0, 0)),
                  idx_spec(lambda i, be, nv, na: (jnp.minimum(i + 1, nb - 1), 0, 0)),
                  idx_spec(lambda i, be, nv, na: (i, 0, 0)),
                  pl.BlockSpec(memory_space=pl.ANY),
                  pl.BlockSpec((None, d, e_dim), lambda i, be, nv, na: (be[i], 0, 0)),
                  pl.BlockSpec((None, d, e_dim), lambda i, be, nv, na: (be[i], 0, 0)),
                  pl.BlockSpec((None, e_dim, d), lambda i, be, nv, na: (be[i], 0, 0))],
        out_specs=pl.BlockSpec(memory_space=pl.ANY),
        scratch_shapes=[pltpu.VMEM((2, MOE_BLOCK, d), F32),
                        pltpu.VMEM((MOE_BLOCK, d), F32),
                        pltpu.SemaphoreType.DMA((2,)),
                        pltpu.SemaphoreType.DMA(())],
    )
    yk = pl.pallas_call(
        _expert_kernel,
        grid_spec=grid_spec,
        out_shape=jax.ShapeDtypeStruct((TOP_K * t, d), F32),
        compiler_params=_params(1),
        name="routed_experts",
    )(block_e, n_valid, n_active, slot_tok, slot_tok, slot_dst, h, wg, wu, wd)
    return yk.reshape(TOP_K, t, d)


def _shared_kernel(h_ref, wg_ref, wu_ref, wd_ref, o_ref):
    x = h_ref[...].astype(BF16)
    g = jnp.dot(x, wg_ref[...], preferred_element_type=F32)
    u = jnp.dot(x, wu_ref[...], preferred_element_type=F32)
    act = (g * _sigmoid(g) * u).astype(BF16)
    o_ref[...] = jnp.dot(act, wd_ref[...], preferred_element_type=F32)


def _shared_expert(h, wg, wu, wd):
    t, d = h.shape
    e_dim = wg.shape[-1]
    tm = min(SHARED_TM, t)
    rowt = pl.BlockSpec((tm, d), lambda i: (i, 0))
    return pl.pallas_call(
        _shared_kernel,
        grid=(t // tm,),
        in_specs=[rowt,
                  pl.BlockSpec((d, e_dim), lambda i: (0, 0)),
                  pl.BlockSpec((d, e_dim), lambda i: (0, 0)),
                  pl.BlockSpec((e_dim, d), lambda i: (0, 0))],
        out_specs=rowt,
        out_shape=jax.ShapeDtypeStruct((t, d), F32),
        compiler_params=_params(1),
        name="shared_expert",
    )(h, wg, wu, wd)


def _final_kernel(yk_ref, h_ref, s_ref, tw_ref, lw_ref, lb_ref, o_ref):
    tw = tw_ref[...]
    routed = tw[:, 0:1] * yk_ref[0]
    for k in range(1, TOP_K):
        routed = routed + tw[:, k:k + 1] * yk_ref[k]
    y = DN_ALPHA * h_ref[...] + (s_ref[...] + routed)
    o_ref[...] = _layernorm(y, lw_ref[...], lb_ref[...])


def _final(yk, h, shared, top_w, ln_w, ln_b):
    t, d = h.shape
    tm = min(FINAL_TM, t)
    rowt = pl.BlockSpec((tm, d), lambda i: (i, 0))
    vec = pl.BlockSpec((1, d), lambda i: (0, 0))
    return pl.pallas_call(
        _final_kernel,
        grid=(t // tm,),
        in_specs=[pl.BlockSpec((TOP_K, tm, d), lambda i: (0, i, 0)),
                  rowt, rowt,
                  pl.BlockSpec((tm, LANES), lambda i: (i, 0)),
                  vec, vec],
        out_specs=rowt,
        out_shape=jax.ShapeDtypeStruct((t, d), F32),
        compiler_params=_params(1),
        name="combine_ln2",
    )(yk, h, shared, top_w, ln_w, ln_b)


def _rotary_tables(positions):
    pos = positions.reshape(-1).astype(F32)[:, None]
    ret_freqs = 1.0 / (RET_THETA ** jnp.linspace(0.0, 1.0, RET_QK_DIM // 2, dtype=F32))
    ang = pos * ret_freqs
    ret_rot = (jnp.cos(ang), jnp.sin(ang))
    rope_freqs = ROPE_THETA ** (-jnp.arange(0, DIFF_QK_DIM, 2, dtype=F32) / DIFF_QK_DIM)
    ang = pos * rope_freqs
    cos, sin = jnp.cos(ang), jnp.sin(ang)
    diff_rot = (jnp.concatenate([cos, cos], axis=-1), jnp.concatenate([-sin, sin], axis=-1))
    return ret_rot, diff_rot


def _layer(h, positions, w_in, w_ret_proj, w_diff_proj, w_out, lq1, lk1, lq2, lk2, diff_norm_w,
           ln1_w, ln1_b, w_router, router_bias, exp_gate, exp_up, exp_down,
           shared_gate, shared_up, shared_down, ln2_w, ln2_b, lambda_init):
    batch, seq, d = h.shape
    t = batch * seq
    x = h.reshape(t, d)
    xb = x.astype(BF16)
    w_in = w_in.astype(BF16)
    ret_rot, diff_rot = _rotary_tables(positions)

    off = 0
    rq = _proj(xb, w_in, off, RET_QW, "ret_rot", 1.0, ret_rot); off += RET_QW
    rk = _proj(xb, w_in, off, RET_QW, "ret_rot", RET_QK_DIM ** -0.5, ret_rot); off += RET_QW
    rv = _proj(xb, w_in, off, RET_VW, "plain"); off += RET_VW
    rg = _proj(xb, w_in, off, RET_VW, "silu"); off += RET_VW
    dq = _proj(xb, w_in, off, DIFF_QW, "diff_rot", DIFF_QK_DIM ** -0.5, diff_rot); off += DIFF_QW
    dk = _proj(xb, w_in, off, DIFF_QW, "diff_rot", 1.0, diff_rot); off += DIFF_QW
    dv = _proj(xb, w_in, off, DIFF_VW, "plain"); off += DIFF_VW
    gate_ret = _proj(xb, w_in, off, d, "sigmoid"); off += d
    gate_diff = _proj(xb, w_in, off, d, "sigmoid")

    ret = _retention(rq, rk, rv, rg, batch, seq)
    row = lambda v: v.reshape(1, -1).astype(F32)
    da = _diff_attention(dq, dk, dv, row(lq1), row(lk1), row(lq2), row(lk2), row(diff_norm_w),
                         batch, seq, lambda_init)
    merged = _merge(ret, da, w_ret_proj.astype(BF16), w_diff_proj.astype(BF16),
                    gate_ret, gate_diff)
    y1 = _outproj(merged, w_out.astype(BF16), x)
    h1, top_e, top_w = _ln_route(y1, row(ln1_w), row(ln1_b), w_router, row(router_bias))

    plan = _expert_plan(top_e[:, :TOP_K], t)
    yk = _routed_experts(h1, plan, exp_gate.astype(BF16), exp_up.astype(BF16),
                         exp_down.astype(BF16))
    shared = _shared_expert(h1, shared_gate.astype(BF16), shared_up.astype(BF16),
                            shared_down.astype(BF16))
    out = _final(yk, h1, shared, top_w, row(ln2_w), row(ln2_b))
    return out.reshape(batch, seq, d)


def kernel(x, positions, w_in, w_ret_proj, w_diff_proj, w_out, lambda_q1, lambda_k1, lambda_q2,
           lambda_k2, diff_norm_w, ln1_w, ln1_b, w_router, router_bias, exp_gate, exp_up,
           exp_down, shared_gate, shared_up, shared_down, ln2_w, ln2_b):
    h = x
    for l in range(w_in.shape[0]):
        lambda_init = 0.8 - 0.6 * math.exp(-0.3 * l)
        h = _layer(h, positions, w_in[l], w_ret_proj[l], w_diff_proj[l], w_out[l],
                   lambda_q1[l], lambda_k1[l], lambda_q2[l], lambda_k2[l], diff_norm_w[l],
                   ln1_w[l], ln1_b[l], w_router[l], router_bias[l], exp_gate[l], exp_up[l],
                   exp_down[l], shared_gate[l], shared_up[l], shared_down[l],
                   ln2_w[l], ln2_b[l], lambda_init)
    return h
```

```python
import functools
import math

import jax
import jax.numpy as jnp
from jax import lax
from jax.experimental import pallas as pl
from jax.experimental.pallas import tpu as pltpu

F32 = jnp.float32
BF16 = jnp.bfloat16

D_MODEL = 4096
CHUNK = 64
RET_HEADS = 8
RET_QK_DIM = 256
RET_V_DIM = 512
RET_THETA = 10000.0
DIFF_HEADS = 16
DIFF_QK_DIM = 128
DIFF_V_DIM = 256
ROPE_THETA = 10000.0
RET_QW = RET_HEADS * RET_QK_DIM
RET_VW = RET_HEADS * RET_V_DIM
DIFF_QW = DIFF_HEADS * 2 * DIFF_QK_DIM
DIFF_VW = DIFF_HEADS * DIFF_V_DIM
N_EXPERTS = 64
TOP_K = 8
N_GROUPS = 8
TOPK_GROUPS = 4
EXPERT_DIM = 512
ROUTED_SCALE = 2.5
DEPTH = 1
DN_ALPHA = (2.0 * DEPTH) ** 0.25
EPS = 1e-5

LANES = 128
SUBLANES = 8
MIB = 1024 * 1024
VMEM_LIMIT = 56 * MIB

PROJ_TM, PROJ_TN = 1024, 512
MERGE_TM, MERGE_TN = 512, 512
RET_BLOCK = 256
ATT_BLOCK = 512
LN_TM = 256
MOE_BLOCK = 256
DISPATCH_TM = 128
SHARED_TM = 256
FINAL_TM = 128


def _params(n_axes):
    return pltpu.CompilerParams(dimension_semantics=("arbitrary",) * n_axes,
                                vmem_limit_bytes=VMEM_LIMIT)


def _sigmoid(v):
    return 1.0 / (1.0 + jnp.exp(-v))


def _proj_kernel(*refs, mode, scale, tn):
    if mode in ("ret_rot", "diff_rot"):
        x_ref, w_ref, c_ref, s_ref, o_ref = refs
    else:
        x_ref, w_ref, o_ref = refs
    acc = jnp.dot(x_ref[...], w_ref[...], preferred_element_type=F32)
    if mode == "ret_rot":
        cos, sin = c_ref[...], s_ref[...]
        for h in range(tn // RET_QK_DIM):
            lo = h * RET_QK_DIM
            x1 = acc[:, lo:lo + LANES]
            x2 = acc[:, lo + LANES:lo + 2 * LANES]
            o_ref[:, lo:lo + LANES] = ((x1 * cos - x2 * sin) * scale).astype(o_ref.dtype)
            o_ref[:, lo + LANES:lo + 2 * LANES] = ((x2 * cos + x1 * sin) * scale).astype(o_ref.dtype)
    elif mode == "diff_rot":
        c, s = c_ref[...], s_ref[...]
        for g in range(tn // LANES):
            xg = acc[:, g * LANES:(g + 1) * LANES]
            rot = pltpu.roll(xg, LANES // 2, axis=1)
            o_ref[:, g * LANES:(g + 1) * LANES] = ((xg * c + rot * s) * scale).astype(o_ref.dtype)
    elif mode == "silu":
        o_ref[...] = (acc * _sigmoid(acc)).astype(o_ref.dtype)
    elif mode == "sigmoid":
        o_ref[...] = _sigmoid(acc).astype(o_ref.dtype)
    else:
        o_ref[...] = acc.astype(o_ref.dtype)


def _proj(xb, w, col_off, width, mode, scale=1.0, rot=None):
    t, d = xb.shape
    tm, tn = min(PROJ_TM, t), PROJ_TN
    joff = col_off // tn
    in_specs = [pl.BlockSpec((tm, d), lambda i, j: (i, 0)),
                pl.BlockSpec((d, tn), lambda i, j: (0, j + joff))]
    args = [xb, w]
    if rot is not None:
        in_specs += [pl.BlockSpec((tm, LANES), lambda i, j: (i, 0))] * 2
        args += list(rot)
    return pl.pallas_call(
        functools.partial(_proj_kernel, mode=mode, scale=scale, tn=tn),
        grid=(t // tm, width // tn),
        in_specs=in_specs,
        out_specs=pl.BlockSpec((tm, tn), lambda i, j: (i, j)),
        out_shape=jax.ShapeDtypeStruct((t, width), BF16),
        compiler_params=_params(2),
        name="proj_" + mode,
    )(*args)


def _ret_kernel(q_ref, k_ref, v_ref, g_ref, dm_ref, qd_ref, kd_ref, cd_ref, o_ref, r_ref):
    @pl.when(pl.program_id(2) == 0)
    def _():
        r_ref[...] = jnp.zeros_like(r_ref)

    q, k, v = q_ref[...], k_ref[...], v_ref[...]
    s = lax.dot_general(q, k, (((1,), (1,)), ((), ())), preferred_element_type=F32)
    s = (s * dm_ref[0]).astype(BF16)
    inner = jnp.dot(s, v, preferred_element_type=F32)
    qs = (q.astype(F32) * qd_ref[0]).astype(BF16)
    r = r_ref[...]
    cross = jnp.dot(qs, r.astype(BF16), preferred_element_type=F32)
    ks = (k.astype(F32) * kd_ref[0]).astype(BF16)
    r_ref[...] = r * cd_ref[0] + lax.dot_general(
        ks, v, (((0,), (0,)), ((), ())), preferred_element_type=F32)
    y = inner + cross
    mu = jnp.mean(y, axis=-1, keepdims=True)
    yc = y - mu
    var = jnp.mean(yc * yc, axis=-1, keepdims=True)
    o_ref[...] = (yc * lax.rsqrt(var + EPS) * g_ref[...].astype(F32)).astype(o_ref.dtype)


def _retention_tables(blk):
    log_g = jnp.log1p(-jnp.exp2(-5.0 - jnp.arange(RET_HEADS, dtype=F32)))
    idx = jnp.arange(blk, dtype=F32)
    chunk = jnp.arange(blk) // CHUNK
    visible = chunk[None, :] <= chunk[:, None]
    dm = jnp.where(visible[None],
                   jnp.exp(log_g[:, None, None] * jnp.abs(idx[:, None] - idx[None, :])), 0.0)
    qd = jnp.exp(log_g[:, None] * (idx + 1.0))
    kd = jnp.exp(log_g[:, None] * (blk - 1.0 - idx))
    cd = jnp.exp(log_g * blk)
    qd = jnp.broadcast_to(qd[:, :, None], (RET_HEADS, blk, RET_QK_DIM))
    kd = jnp.broadcast_to(kd[:, :, None], (RET_HEADS, blk, RET_QK_DIM))
    cd = jnp.broadcast_to(cd[:, None, None], (RET_HEADS, 1, RET_V_DIM))
    return dm, qd, kd, cd


def _retention(rq, rk, rv, rg, batch, seq):
    t = batch * seq
    blk = min(RET_BLOCK, seq)
    nl = seq // blk
    dm, qd, kd, cd = _retention_tables(blk)
    row = lambda b, h, l: (b * nl + l, h)
    head = lambda b, h, l: (h, 0, 0)
    return pl.pallas_call(
        _ret_kernel,
        grid=(batch, RET_HEADS, nl),
        in_specs=[pl.BlockSpec((blk, RET_QK_DIM), row),
                  pl.BlockSpec((blk, RET_QK_DIM), row),
                  pl.BlockSpec((blk, RET_V_DIM), row),
                  pl.BlockSpec((blk, RET_V_DIM), row),
                  pl.BlockSpec((1, blk, blk), head),
                  pl.BlockSpec((1, blk, RET_QK_DIM), head),
                  pl.BlockSpec((1, blk, RET_QK_DIM), head),
                  pl.BlockSpec((1, 1, RET_V_DIM), head)],
        out_specs=pl.BlockSpec((blk, RET_V_DIM), row),
        out_shape=jax.ShapeDtypeStruct((t, RET_VW), BF16),
        scratch_shapes=[pltpu.VMEM((RET_QK_DIM, RET_V_DIM), F32)],
        compiler_params=_params(3),
        name="retention",
    )(rq, rk, rv, rg, dm, qd, kd, cd)


def _attn_kernel(lq1_ref, lk1_ref, lq2_ref, lk2_ref, nw_ref, q_ref, k_ref, v_ref, o_ref,
                 m_ref, l_ref, acc_ref, *, blk, lambda_init):
    qi = pl.program_id(2)
    m_ref[...] = jnp.full_like(m_ref, -jnp.inf)
    l_ref[...] = jnp.zeros_like(l_ref)
    acc_ref[...] = jnp.zeros_like(acc_ref)
    q = q_ref[...]

    def block(kj, masked):
        off = pl.multiple_of(kj * blk, blk)
        kb = k_ref[pl.ds(off, blk), :]
        vb = v_ref[pl.ds(off, blk), :]
        if masked:
            rchunk = lax.broadcasted_iota(jnp.int32, (blk, blk), 0) // CHUNK
            cchunk = lax.broadcasted_iota(jnp.int32, (blk, blk), 1) // CHUNK
            visible = cchunk <= rchunk
        for c in range(2):
            qc = q[:, c * DIFF_QK_DIM:(c + 1) * DIFF_QK_DIM]
            kc = kb[:, c * DIFF_QK_DIM:(c + 1) * DIFF_QK_DIM]
            s = lax.dot_general(qc, kc, (((1,), (1,)), ((), ())), preferred_element_type=F32)
            if masked:
                s = jnp.where(visible, s, -jnp.inf)
            m_prev = m_ref[c]
            m_new = jnp.maximum(m_prev, jnp.max(s, axis=-1, keepdims=True))
            p = jnp.exp(s - jnp.tile(m_new, (1, blk // LANES)))
            alpha = jnp.exp(m_prev - m_new)
            l_ref[c] = alpha * l_ref[c] + jnp.sum(p, axis=-1, keepdims=True)
            acc_ref[c] = (jnp.tile(alpha, (1, DIFF_V_DIM // LANES)) * acc_ref[c]
                          + jnp.dot(p.astype(BF16), vb, preferred_element_type=F32))
            m_ref[c] = m_new

    def body(kj, carry):
        block(kj, False)
        return carry

    lax.fori_loop(0, qi, body, 0)
    block(qi, True)

    lam = (jnp.exp(jnp.sum(lq1_ref[...] * lk1_ref[...], axis=-1, keepdims=True))
           - jnp.exp(jnp.sum(lq2_ref[...] * lk2_ref[...], axis=-1, keepdims=True))
           + lambda_init)
    rep = DIFF_V_DIM // LANES
    o = (acc_ref[0] / jnp.tile(l_ref[0], (1, rep))
         - lam * (acc_ref[1] / jnp.tile(l_ref[1], (1, rep))))
    ms = jnp.mean(o * o, axis=-1, keepdims=True)
    o_ref[...] = (o * lax.rsqrt(ms + EPS) * nw_ref[...] * (1.0 - lambda_init)).astype(o_ref.dtype)


def _diff_attention(dq, dk, dv, lq1, lk1, lq2, lk2, norm_w, batch, seq, lambda_init):
    t = batch * seq
    blk = min(ATT_BLOCK, seq)
    nq = seq // blk
    hw = 2 * DIFF_QK_DIM
    vec = lambda b, h, i: (0, 0)
    return pl.pallas_call(
        functools.partial(_attn_kernel, blk=blk, lambda_init=lambda_init),
        grid=(batch, DIFF_HEADS, nq),
        in_specs=[pl.BlockSpec((1, DIFF_QK_DIM), vec)] * 4
        + [pl.BlockSpec((1, DIFF_V_DIM), vec),
           pl.BlockSpec((blk, hw), lambda b, h, i: (b * nq + i, h)),
           pl.BlockSpec((seq, hw), lambda b, h, i: (b, h)),
           pl.BlockSpec((seq, DIFF_V_DIM), lambda b, h, i: (b, h))],
        out_specs=pl.BlockSpec((blk, DIFF_V_DIM), lambda b, h, i: (b * nq + i, h)),
        out_shape=jax.ShapeDtypeStruct((t, DIFF_VW), BF16),
        scratch_shapes=[pltpu.VMEM((2, blk, LANES), F32),
                        pltpu.VMEM((2, blk, LANES), F32),
                        pltpu.VMEM((2, blk, DIFF_V_DIM), F32)],
        compiler_params=_params(3),
        name="diff_attention",
    )(lq1, lk1, lq2, lk2, norm_w, dq, dk, dv)


def _merge_kernel(r_ref, d_ref, wr_ref, wd_ref, gr_ref, gd_ref, o_ref):
    a = jnp.dot(r_ref[...], wr_ref[...], preferred_element_type=F32)
    b = jnp.dot(d_ref[...], wd_ref[...], preferred_element_type=F32)
    o_ref[...] = (gr_ref[...].astype(F32) * a + gd_ref[...].astype(F32) * b).astype(o_ref.dtype)


def _merge(ret, da, wr, wd, gr, gd):
    t, kdim = ret.shape
    n = wr.shape[1]
    tm, tn = min(MERGE_TM, t), MERGE_TN
    lhs = pl.BlockSpec((tm, kdim), lambda i, j: (i, 0))
    rhs = pl.BlockSpec((kdim, tn), lambda i, j: (0, j))
    tile = pl.BlockSpec((tm, tn), lambda i, j: (i, j))
    return pl.pallas_call(
        _merge_kernel,
        grid=(t // tm, n // tn),
        in_specs=[lhs, lhs, rhs, rhs, tile, tile],
        out_specs=tile,
        out_shape=jax.ShapeDtypeStruct((t, n), BF16),
        compiler_params=_params(2),
        name="merge",
    )(ret, da, wr, wd, gr, gd)


def _outproj_kernel(m_ref, w_ref, x_ref, o_ref):
    o_ref[...] = DN_ALPHA * x_ref[...] + jnp.dot(m_ref[...], w_ref[...],
                                                  preferred_element_type=F32)


def _outproj(merged, w, x):
    t, kdim = merged.shape
    n = w.shape[1]
    tm, tn = min(PROJ_TM, t), PROJ_TN
    tile = pl.BlockSpec((tm, tn), lambda i, j: (i, j))
    return pl.pallas_call(
        _outproj_kernel,
        grid=(t // tm, n // tn),
        in_specs=[pl.BlockSpec((tm, kdim), lambda i, j: (i, 0)),
                  pl.BlockSpec((kdim, tn), lambda i, j: (0, j)),
                  tile],
        out_specs=tile,
        out_shape=jax.ShapeDtypeStruct((t, n), F32),
        compiler_params=_params(2),
        name="outproj",
    )(merged, w, x)


def _layernorm(y, w, b):
    mu = jnp.mean(y, axis=-1, keepdims=True)
    yc = y - mu
    var = jnp.mean(yc * yc, axis=-1, keepdims=True)
    return yc * lax.rsqrt(var + EPS) * w + b


def _ln_route_kernel(y_ref, lw_ref, lb_ref, wr_ref, rb_ref, h_ref, te_ref, tw_ref, tr_ref, tc_ref,
                     cnt_ref):
    h = _layernorm(y_ref[...], lw_ref[...], lb_ref[...])
    h_ref[...] = h
    hi = h.astype(BF16)
    lo = (h - hi.astype(F32)).astype(BF16)
    w = wr_ref[...]
    whi = w.astype(BF16)
    wlo = (w - whi.astype(F32)).astype(BF16)
    logits = (jnp.dot(hi, whi, preferred_element_type=F32)
              + (jnp.dot(hi, wlo, preferred_element_type=F32)
                 + jnp.dot(lo, whi, preferred_element_type=F32)))
    scores = _sigmoid(logits)
    biased = scores + rb_ref[...]
    tm = scores.shape[0]
    neg = -jnp.inf
    lane_i = lax.broadcasted_iota(jnp.int32, (tm, N_EXPERTS), 1)
    lane = lane_i.astype(F32)
    per_group = N_EXPERTS // N_GROUPS
    grp = lane_i // per_group

    gscores = []
    gs_lane = jnp.zeros((tm, N_EXPERTS), F32)
    for g in range(N_GROUPS):
        mk = grp == g
        vals = jnp.where(mk, biased, neg)
        m1 = jnp.max(vals, axis=-1, keepdims=True)
        i1 = jnp.min(jnp.where(vals == m1, lane, float(N_EXPERTS)), axis=-1, keepdims=True)
        m2 = jnp.max(jnp.where(lane == i1, neg, vals), axis=-1, keepdims=True)
        gscores.append(m1 + m2)
        gs_lane = jnp.where(mk, m1 + m2, gs_lane)
    beaten = jnp.zeros((tm, N_EXPERTS), F32)
    for g in range(N_GROUPS):
        wins = jnp.where(gscores[g] > gs_lane, 1.0,
                         jnp.where(gscores[g] == gs_lane,
                                   jnp.where(grp > g, 1.0, 0.0), 0.0))
        beaten = beaten + wins
    cur = jnp.where(beaten < float(TOPK_GROUPS), biased, neg)

    out_lane = lax.broadcasted_iota(jnp.int32, (tm, LANES), 1)
    sel_e = jnp.zeros((tm, LANES), F32)
    sel_w = jnp.zeros((tm, LANES), F32)
    wsum = jnp.zeros((tm, 1), F32)
    hits = []
    for k in range(TOP_K):
        m = jnp.max(cur, axis=-1, keepdims=True)
        idx = jnp.min(jnp.where(cur == m, lane, float(N_EXPERTS)), axis=-1, keepdims=True)
        hit = lane == idx
        wk = jnp.sum(jnp.where(hit, scores, 0.0), axis=-1, keepdims=True)
        cur = jnp.where(hit, neg, cur)
        wsum = wsum + wk
        sel_e = jnp.where(out_lane == k, idx, sel_e)
        sel_w = jnp.where(out_lane == k, wk, sel_w)
        hits.append(hit)
    te_ref[...] = sel_e.astype(jnp.int32)
    tw_ref[...] = sel_w / wsum * ROUTED_SCALE

    @pl.when(pl.program_id(0) == 0)
    def _():
        cnt_ref[...] = jnp.zeros_like(cnt_ref)

    chosen = jnp.zeros((tm, N_EXPERTS), F32)
    for hit in hits:
        chosen = chosen + jnp.where(hit, 1.0, 0.0)
    earlier = (lax.broadcasted_iota(jnp.int32, (tm, tm), 0)
               > lax.broadcasted_iota(jnp.int32, (tm, tm), 1))
    prefix = jnp.dot(jnp.where(earlier, 1.0, 0.0).astype(BF16), chosen.astype(BF16),
                     preferred_element_type=F32) + cnt_ref[...]
    sel_r = jnp.zeros((tm, LANES), F32)
    for k, hit in enumerate(hits):
        rk = jnp.sum(jnp.where(hit, prefix, 0.0), axis=-1, keepdims=True)
        sel_r = jnp.where(out_lane == k, rk, sel_r)
    tr_ref[...] = sel_r.astype(jnp.int32)
    cnt_ref[...] = cnt_ref[...] + jnp.sum(chosen, axis=0, keepdims=True)
    tc_ref[...] = cnt_ref[...]


def _ln_route(y, ln_w, ln_b, w_router, router_bias):
    t, d = y.shape
    tm = min(LN_TM, t)
    rowt = pl.BlockSpec((tm, d), lambda i: (i, 0))
    vec = pl.BlockSpec((1, d), lambda i: (0, 0))
    narrow = pl.BlockSpec((tm, LANES), lambda i: (i, 0))
    return pl.pallas_call(
        _ln_route_kernel,
        grid=(t // tm,),
        in_specs=[rowt, vec, vec,
                  pl.BlockSpec((d, N_EXPERTS), lambda i: (0, 0)),
                  pl.BlockSpec((1, N_EXPERTS), lambda i: (0, 0))],
        out_specs=[rowt, narrow, narrow, narrow,
                   pl.BlockSpec((1, N_EXPERTS), lambda i: (0, 0))],
        out_shape=[jax.ShapeDtypeStruct((t, d), F32),
                   jax.ShapeDtypeStruct((t, LANES), jnp.int32),
                   jax.ShapeDtypeStruct((t, LANES), F32),
                   jax.ShapeDtypeStruct((t, LANES), jnp.int32),
                   jax.ShapeDtypeStruct((1, N_EXPERTS), F32)],
        scratch_shapes=[pltpu.VMEM((1, N_EXPERTS), F32)],
        compiler_params=_params(1),
        name="ln_route",
    )(y, ln_w, ln_b, w_router, router_bias)


def _expert_plan(top_e, rank, counts, t):
    nb = t * TOP_K // MOE_BLOCK + N_EXPERTS
    counts = counts.reshape(N_EXPERTS).astype(jnp.int32)
    padded = (counts + MOE_BLOCK - 1) // MOE_BLOCK * MOE_BLOCK
    pad_end = jnp.cumsum(padded)
    pad_start = pad_end - padded
    experts = jnp.arange(N_EXPERTS, dtype=jnp.int32)
    dest = jnp.sum(jnp.where(top_e[..., None] == experts, pad_start, 0), axis=-1) + rank
    n_active = (pad_end[-1] // MOE_BLOCK).astype(jnp.int32)
    blk_start = jnp.arange(nb, dtype=jnp.int32) * MOE_BLOCK
    block_e = jnp.sum((pad_end[None, :] <= blk_start[:, None]).astype(jnp.int32), axis=-1)
    block_e = jnp.minimum(block_e, N_EXPERTS - 1)
    last_e = jnp.sum((pad_end <= (n_active - 1) * MOE_BLOCK).astype(jnp.int32))
    block_e = jnp.where(jnp.arange(nb) < n_active, block_e, jnp.minimum(last_e, N_EXPERTS - 1))
    tail_pieces = (nb * MOE_BLOCK - pad_end[-1]) // (MOE_BLOCK // 2)
    zero_start = jnp.concatenate([pad_start + counts, pad_end[-1:]]).astype(jnp.int32)
    zero_cnt = jnp.concatenate([padded - counts, tail_pieces[None]]).astype(jnp.int32)
    return (dest.astype(jnp.int32), block_e.astype(jnp.int32), n_active.reshape(1),
            zero_start, zero_cnt)


def _dispatch_kernel(zs_ref, zn_ref, dest_ref, h_ref, xs_hbm, buf, zbuf, sem, zsem, *, tm):
    i = pl.program_id(0)
    n = pl.num_programs(0)
    slot = lax.rem(i, 2)

    def zero_rows(wait):
        def zero_copy(off, rows):
            cp = pltpu.make_async_copy(zbuf.at[pl.ds(0, rows)], xs_hbm.at[pl.ds(off, rows)], zsem)
            if wait:
                cp.wait()
            else:
                cp.start()

        def single_rows(off, count):
            def one(j, carry):
                zero_copy(off + j, 1)
                return carry
            lax.fori_loop(0, count, one, 0)

        def body(e, carry):
            cnt = zn_ref[e]
            start = zs_ref[e]
            head = jnp.minimum((-start) & (SUBLANES - 1), cnt)
            single_rows(start, head)
            rest = cnt - head
            off = start + head
            rows = MOE_BLOCK // 2
            while rows >= SUBLANES:
                @pl.when((rest & rows) != 0)
                def _():
                    zero_copy(pl.multiple_of(off, SUBLANES), rows)
                off = off + (rest & rows)
                rows //= 2
            single_rows(off, rest & (SUBLANES - 1))
            return carry
        lax.fori_loop(0, N_EXPERTS, body, 0)

        def tail(c, carry):
            zero_copy(pl.multiple_of(zs_ref[N_EXPERTS] + c * (MOE_BLOCK // 2), MOE_BLOCK // 2),
                      MOE_BLOCK // 2)
            return carry
        lax.fori_loop(0, zn_ref[N_EXPERTS], tail, 0)

    @pl.when(i == 0)
    def _():
        zbuf[...] = jnp.zeros_like(zbuf)
        zero_rows(False)
        zero_rows(True)

    def wait_rows(s):
        for _ in range(TOP_K):
            pltpu.make_async_copy(buf.at[s], xs_hbm.at[pl.ds(0, tm)], sem.at[s]).wait()

    @pl.when(i >= 2)
    def _():
        wait_rows(slot)

    buf[slot] = h_ref[...]

    def body(r, carry):
        for k in range(TOP_K):
            pltpu.make_async_copy(buf.at[slot, pl.ds(r, 1)],
                                  xs_hbm.at[pl.ds(dest_ref[0, 0, r * TOP_K + k], 1)],
                                  sem.at[slot]).start()
        return carry
    lax.fori_loop(0, tm, body, 0)

    @pl.when(i == n - 1)
    def _():
        wait_rows(slot)

    @pl.when((i == n - 1) & (n >= 2))
    def _():
        wait_rows(1 - slot)


def _dispatch(h, dest, zero_start, zero_cnt, nb):
    t, d = h.shape
    tm = min(DISPATCH_TM, t)
    grid_spec = pltpu.PrefetchScalarGridSpec(
        num_scalar_prefetch=2,
        grid=(t // tm,),
        in_specs=[pl.BlockSpec((1, 1, tm * TOP_K), lambda i, zs, zn: (i, 0, 0),
                               memory_space=pltpu.SMEM),
                  pl.BlockSpec((tm, d), lambda i, zs, zn: (i, 0))],
        out_specs=pl.BlockSpec(memory_space=pl.ANY),
        scratch_shapes=[pltpu.VMEM((2, tm, d), F32),
                        pltpu.VMEM((MOE_BLOCK // 2, d), F32),
                        pltpu.SemaphoreType.DMA((2,)),
                        pltpu.SemaphoreType.DMA(())],
    )
    return pl.pallas_call(
        functools.partial(_dispatch_kernel, tm=tm),
        grid_spec=grid_spec,
        out_shape=jax.ShapeDtypeStruct((nb * MOE_BLOCK, d), F32),
        compiler_params=_params(1),
        name="dispatch",
    )(zero_start, zero_cnt, dest.reshape(t // tm, 1, tm * TOP_K), h)


def _expert_kernel(be_ref, na_ref, x_ref, wg_ref, wu_ref, wd_ref, o_ref):
    i = pl.program_id(0)

    @pl.when(i < na_ref[0])
    def _():
        x = x_ref[...].astype(BF16)
        g = jnp.dot(x, wg_ref[...], preferred_element_type=F32)
        u = jnp.dot(x, wu_ref[...], preferred_element_type=F32)
        act = (g * _sigmoid(g) * u).astype(BF16)
        o_ref[...] = jnp.dot(act, wd_ref[...], preferred_element_type=F32)

    @pl.when(i >= na_ref[0])
    def _():
        o_ref[...] = jnp.zeros_like(o_ref)


def _routed_experts(xs, block_e, n_active, wg, wu, wd):
    p, d = xs.shape
    nb = p // MOE_BLOCK
    e_dim = wg.shape[-1]
    expert = lambda i, be, na: (be[i], 0, 0)
    grid_spec = pltpu.PrefetchScalarGridSpec(
        num_scalar_prefetch=2,
        grid=(nb,),
        in_specs=[pl.BlockSpec((MOE_BLOCK, d),
                               lambda i, be, na: (jnp.minimum(i, jnp.maximum(na[0] - 1, 0)), 0)),
                  pl.BlockSpec((None, d, e_dim), expert),
                  pl.BlockSpec((None, d, e_dim), expert),
                  pl.BlockSpec((None, e_dim, d), expert)],
        out_specs=pl.BlockSpec((MOE_BLOCK, d), lambda i, be, na: (i, 0)),
    )
    return pl.pallas_call(
        _expert_kernel,
        grid_spec=grid_spec,
        out_shape=jax.ShapeDtypeStruct((p, d), F32),
        compiler_params=_params(1),
        name="routed_experts",
    )(block_e, n_active, xs, wg, wu, wd)


def _shared_kernel(h_ref, wg_ref, wu_ref, wd_ref, o_ref):
    x = h_ref[...].astype(BF16)
    g = jnp.dot(x, wg_ref[...], preferred_element_type=F32)
    u = jnp.dot(x, wu_ref[...], preferred_element_type=F32)
    act = (g * _sigmoid(g) * u).astype(BF16)
    o_ref[...] = jnp.dot(act, wd_ref[...], preferred_element_type=F32)


def _shared_expert(h, wg, wu, wd):
    t, d = h.shape
    e_dim = wg.shape[-1]
    tm = min(SHARED_TM, t)
    rowt = pl.BlockSpec((tm, d), lambda i: (i, 0))
    return pl.pallas_call(
        _shared_kernel,
        grid=(t // tm,),
        in_specs=[rowt,
                  pl.BlockSpec((d, e_dim), lambda i: (0, 0)),
                  pl.BlockSpec((d, e_dim), lambda i: (0, 0)),
                  pl.BlockSpec((e_dim, d), lambda i: (0, 0))],
        out_specs=rowt,
        out_shape=jax.ShapeDtypeStruct((t, d), F32),
        compiler_params=_params(1),
        name="shared_expert",
    )(h, wg, wu, wd)


def _final_kernel(dest_ref, destn_ref, ys_hbm, h_ref, s_ref, tw_ref, lw_ref, lb_ref, o_ref,
                  gbuf, sem, *, tm):
    i = pl.program_id(0)
    n = pl.num_programs(0)
    slot = lax.rem(i, 2)

    def gather_start(idx_ref, s):
        def body(r, carry):
            for k in range(TOP_K):
                pltpu.make_async_copy(ys_hbm.at[pl.ds(idx_ref[0, 0, r * TOP_K + k], 1)],
                                      gbuf.at[s, k, pl.ds(r, 1)], sem.at[s]).start()
            return carry
        lax.fori_loop(0, tm, body, 0)

    @pl.when(i == 0)
    def _():
        gather_start(dest_ref, 0)

    @pl.when(i + 1 < n)
    def _():
        gather_start(destn_ref, 1 - slot)

    for k in range(TOP_K):
        pltpu.make_async_copy(ys_hbm.at[pl.ds(0, tm)], gbuf.at[slot, k], sem.at[slot]).wait()

    tw = tw_ref[...]
    routed = tw[:, 0:1] * gbuf[slot, 0]
    for k in range(1, TOP_K):
        routed = routed + tw[:, k:k + 1] * gbuf[slot, k]
    y = DN_ALPHA * h_ref[...] + (s_ref[...] + routed)
    o_ref[...] = _layernorm(y, lw_ref[...], lb_ref[...])


def _final(ys, dest, h, shared, top_w, ln_w, ln_b):
    t, d = h.shape
    tm = min(FINAL_TM, t)
    n = t // tm
    idx = dest.reshape(n, 1, tm * TOP_K)
    idx_spec = lambda fn: pl.BlockSpec((1, 1, tm * TOP_K), fn, memory_space=pltpu.SMEM)
    rowt = pl.BlockSpec((tm, d), lambda i: (i, 0))
    vec = pl.BlockSpec((1, d), lambda i: (0, 0))
    return pl.pallas_call(
        functools.partial(_final_kernel, tm=tm),
        grid=(n,),
        in_specs=[idx_spec(lambda i: (i, 0, 0)),
                  idx_spec(lambda i: (jnp.minimum(i + 1, n - 1), 0, 0)),
                  pl.BlockSpec(memory_space=pl.ANY),
                  rowt, rowt,
                  pl.BlockSpec((tm, LANES), lambda i: (i, 0)),
                  vec, vec],
        out_specs=rowt,
        out_shape=jax.ShapeDtypeStruct((t, d), F32),
        scratch_shapes=[pltpu.VMEM((2, TOP_K, tm, d), F32),
                        pltpu.SemaphoreType.DMA((2,))],
        compiler_params=_params(1),
        name="combine_ln2",
    )(idx, idx, ys, h, shared, top_w, ln_w, ln_b)


def _rotary_tables(positions):
    pos = positions.reshape(-1).astype(F32)[:, None]
    ret_freqs = 1.0 / (RET_THETA ** jnp.linspace(0.0, 1.0, RET_QK_DIM // 2, dtype=F32))
    ang = pos * ret_freqs
    ret_rot = (jnp.cos(ang), jnp.sin(ang))
    rope_freqs = ROPE_THETA ** (-jnp.arange(0, DIFF_QK_DIM, 2, dtype=F32) / DIFF_QK_DIM)
    ang = pos * rope_freqs
    cos, sin = jnp.cos(ang), jnp.sin(ang)
    diff_rot = (jnp.concatenate([cos, cos], axis=-1), jnp.concatenate([-sin, sin], axis=-1))
    return ret_rot, diff_rot


def _layer(h, positions, w_in, w_ret_proj, w_diff_proj, w_out, lq1, lk1, lq2, lk2, diff_norm_w,
           ln1_w, ln1_b, w_router, router_bias, exp_gate, exp_up, exp_down,
           shared_gate, shared_up, shared_down, ln2_w, ln2_b, lambda_init):
    batch, seq, d = h.shape
    t = batch * seq
    x = h.reshape(t, d)
    xb = x.astype(BF16)
    w_in = w_in.astype(BF16)
    ret_rot, diff_rot = _rotary_tables(positions)

    off = 0
    rq = _proj(xb, w_in, off, RET_QW, "ret_rot", 1.0, ret_rot); off += RET_QW
    rk = _proj(xb, w_in, off, RET_QW, "ret_rot", RET_QK_DIM ** -0.5, ret_rot); off += RET_QW
    rv = _proj(xb, w_in, off, RET_VW, "plain"); off += RET_VW
    rg = _proj(xb, w_in, off, RET_VW, "silu"); off += RET_VW
    dq = _proj(xb, w_in, off, DIFF_QW, "diff_rot", DIFF_QK_DIM ** -0.5, diff_rot); off += DIFF_QW
    dk = _proj(xb, w_in, off, DIFF_QW, "diff_rot", 1.0, diff_rot); off += DIFF_QW
    dv = _proj(xb, w_in, off, DIFF_VW, "plain"); off += DIFF_VW
    gate_ret = _proj(xb, w_in, off, d, "sigmoid"); off += d
    gate_diff = _proj(xb, w_in, off, d, "sigmoid")

    ret = _retention(rq, rk, rv, rg, batch, seq)
    row = lambda v: v.reshape(1, -1).astype(F32)
    da = _diff_attention(dq, dk, dv, row(lq1), row(lk1), row(lq2), row(lk2), row(diff_norm_w),
                         batch, seq, lambda_init)
    merged = _merge(ret, da, w_ret_proj.astype(BF16), w_diff_proj.astype(BF16),
                    gate_ret, gate_diff)
    y1 = _outproj(merged, w_out.astype(BF16), x)
    h1, top_e, top_w, rank, counts = _ln_route(y1, row(ln1_w), row(ln1_b), w_router,
                                               row(router_bias))

    dest, block_e, n_active, zero_start, zero_cnt = _expert_plan(
        top_e[:, :TOP_K], rank[:, :TOP_K], counts, t)
    xs = _dispatch(h1, dest, zero_start, zero_cnt, block_e.shape[0])
    ys = _routed_experts(xs, block_e, n_active, exp_gate.astype(BF16), exp_up.astype(BF16),
                         exp_down.astype(BF16))
    shared = _shared_expert(h1, shared_gate.astype(BF16), shared_up.astype(BF16),
                            shared_down.astype(BF16))
    out = _final(ys, dest, h1, shared, top_w, row(ln2_w), row(ln2_b))
    return out.reshape(batch, seq, d)


def kernel(x, positions, w_in, w_ret_proj, w_diff_proj, w_out, lambda_q1, lambda_k1, lambda_q2,
           lambda_k2, diff_norm_w, ln1_w, ln1_b, w_router, router_bias, exp_gate, exp_up,
           exp_down, shared_gate, shared_up, shared_down, ln2_w, ln2_b):
    h = x
    for l in range(w_in.shape[0]):
        lambda_init = 0.8 - 0.6 * math.exp(-0.3 * l)
        h = _layer(h, positions, w_in[l], w_ret_proj[l], w_diff_proj[l], w_out[l],
                   lambda_q1[l], lambda_k1[l], lambda_q2[l], lambda_k2[l], diff_norm_w[l],
                   ln1_w[l], ln1_b[l], w_router[l], router_bias[l], exp_gate[l], exp_up[l],
                   exp_down[l], shared_gate[l], shared_up[l], shared_down[l],
                   ln2_w[l], ln2_b[l], lambda_init)
    return h
```

```python
import functools
import math

import jax
import jax.numpy as jnp
from jax import lax
from jax.experimental import pallas as pl
from jax.experimental.pallas import tpu as pltpu

F32 = jnp.float32
BF16 = jnp.bfloat16

D_MODEL = 4096
CHUNK = 64
RET_HEADS = 8
RET_QK_DIM = 256
RET_V_DIM = 512
RET_THETA = 10000.0
DIFF_HEADS = 16
DIFF_QK_DIM = 128
DIFF_V_DIM = 256
ROPE_THETA = 10000.0
RET_QW = RET_HEADS * RET_QK_DIM
RET_VW = RET_HEADS * RET_V_DIM
DIFF_QW = DIFF_HEADS * 2 * DIFF_QK_DIM
DIFF_VW = DIFF_HEADS * DIFF_V_DIM
N_EXPERTS = 64
TOP_K = 8
N_GROUPS = 8
TOPK_GROUPS = 4
EXPERT_DIM = 512
ROUTED_SCALE = 2.5
DEPTH = 1
DN_ALPHA = (2.0 * DEPTH) ** 0.25
EPS = 1e-5
LOG2_E = math.log2(math.e)

LANES = 128
SUBLANES = 8
MIB = 1024 * 1024
VMEM_LIMIT = 56 * MIB

PROJ_TM, PROJ_TN = 1024, 512
MERGE_TM, MERGE_TN = 512, 512
RET_BLOCK = 256
ATT_BLOCK = 512
LN_TM = 256
MOE_BLOCK = 256
DISPATCH_TM = 128
SHARED_TM = 256
FINAL_TM = 128


def _params(n_axes):
    return pltpu.CompilerParams(dimension_semantics=("arbitrary",) * n_axes,
                                vmem_limit_bytes=VMEM_LIMIT)


def _sigmoid(v):
    return 1.0 / (1.0 + jnp.exp(-v))


def _proj_kernel(*refs, mode, scale, tn):
    if mode in ("ret_rot", "diff_rot"):
        x_ref, w_ref, c_ref, s_ref, o_ref = refs
    else:
        x_ref, w_ref, o_ref = refs
    acc = jnp.dot(x_ref[...], w_ref[...], preferred_element_type=F32)
    if mode == "ret_rot":
        cos, sin = c_ref[...], s_ref[...]
        for h in range(tn // RET_QK_DIM):
            lo = h * RET_QK_DIM
            x1 = acc[:, lo:lo + LANES]
            x2 = acc[:, lo + LANES:lo + 2 * LANES]
            o_ref[:, lo:lo + LANES] = ((x1 * cos - x2 * sin) * scale).astype(o_ref.dtype)
            o_ref[:, lo + LANES:lo + 2 * LANES] = ((x2 * cos + x1 * sin) * scale).astype(o_ref.dtype)
    elif mode == "diff_rot":
        c, s = c_ref[...], s_ref[...]
        for g in range(tn // LANES):
            xg = acc[:, g * LANES:(g + 1) * LANES]
            rot = pltpu.roll(xg, LANES // 2, axis=1)
            o_ref[:, g * LANES:(g + 1) * LANES] = ((xg * c + rot * s) * scale).astype(o_ref.dtype)
    elif mode == "silu":
        o_ref[...] = (acc * _sigmoid(acc)).astype(o_ref.dtype)
    elif mode == "sigmoid":
        o_ref[...] = _sigmoid(acc).astype(o_ref.dtype)
    else:
        o_ref[...] = acc.astype(o_ref.dtype)


def _proj(xb, w, col_off, width, mode, scale=1.0, rot=None):
    t, d = xb.shape
    tm, tn = min(PROJ_TM, t), PROJ_TN
    joff = col_off // tn
    in_specs = [pl.BlockSpec((tm, d), lambda i, j: (i, 0)),
                pl.BlockSpec((d, tn), lambda i, j: (0, j + joff))]
    args = [xb, w]
    if rot is not None:
        in_specs += [pl.BlockSpec((tm, LANES), lambda i, j: (i, 0))] * 2
        args += list(rot)
    return pl.pallas_call(
        functools.partial(_proj_kernel, mode=mode, scale=scale, tn=tn),
        grid=(t // tm, width // tn),
        in_specs=in_specs,
        out_specs=pl.BlockSpec((tm, tn), lambda i, j: (i, j)),
        out_shape=jax.ShapeDtypeStruct((t, width), BF16),
        compiler_params=_params(2),
        name="proj_" + mode,
    )(*args)


def _ret_kernel(q_ref, k_ref, v_ref, g_ref, dm_ref, qd_ref, kd_ref, cd_ref, o_ref, r_ref):
    @pl.when(pl.program_id(2) == 0)
    def _():
        r_ref[...] = jnp.zeros_like(r_ref)

    q, k, v = q_ref[...], k_ref[...], v_ref[...]
    s = lax.dot_general(q, k, (((1,), (1,)), ((), ())), preferred_element_type=F32)
    s = (s * dm_ref[0]).astype(BF16)
    inner = jnp.dot(s, v, preferred_element_type=F32)
    qs = (q.astype(F32) * qd_ref[0]).astype(BF16)
    r = r_ref[...]
    cross = jnp.dot(qs, r.astype(BF16), preferred_element_type=F32)
    ks = (k.astype(F32) * kd_ref[0]).astype(BF16)
    r_ref[...] = r * cd_ref[0] + lax.dot_general(
        ks, v, (((0,), (0,)), ((), ())), preferred_element_type=F32)
    y = inner + cross
    mu = jnp.mean(y, axis=-1, keepdims=True)
    yc = y - mu
    var = jnp.mean(yc * yc, axis=-1, keepdims=True)
    o_ref[...] = (yc * lax.rsqrt(var + EPS) * g_ref[...].astype(F32)).astype(o_ref.dtype)


def _retention_tables(blk):
    log_g = jnp.log1p(-jnp.exp2(-5.0 - jnp.arange(RET_HEADS, dtype=F32)))
    idx = jnp.arange(blk, dtype=F32)
    chunk = jnp.arange(blk) // CHUNK
    visible = chunk[None, :] <= chunk[:, None]
    dm = jnp.where(visible[None],
                   jnp.exp(log_g[:, None, None] * jnp.abs(idx[:, None] - idx[None, :])), 0.0)
    qd = jnp.exp(log_g[:, None] * (idx + 1.0))
    kd = jnp.exp(log_g[:, None] * (blk - 1.0 - idx))
    cd = jnp.exp(log_g * blk)
    qd = jnp.broadcast_to(qd[:, :, None], (RET_HEADS, blk, RET_QK_DIM))
    kd = jnp.broadcast_to(kd[:, :, None], (RET_HEADS, blk, RET_QK_DIM))
    cd = jnp.broadcast_to(cd[:, None, None], (RET_HEADS, 1, RET_V_DIM))
    return dm, qd, kd, cd


def _retention(rq, rk, rv, rg, batch, seq):
    t = batch * seq
    blk = min(RET_BLOCK, seq)
    nl = seq // blk
    dm, qd, kd, cd = _retention_tables(blk)
    row = lambda b, h, l: (b * nl + l, h)
    head = lambda b, h, l: (h, 0, 0)
    return pl.pallas_call(
        _ret_kernel,
        grid=(batch, RET_HEADS, nl),
        in_specs=[pl.BlockSpec((blk, RET_QK_DIM), row),
                  pl.BlockSpec((blk, RET_QK_DIM), row),
                  pl.BlockSpec((blk, RET_V_DIM), row),
                  pl.BlockSpec((blk, RET_V_DIM), row),
                  pl.BlockSpec((1, blk, blk), head),
                  pl.BlockSpec((1, blk, RET_QK_DIM), head),
                  pl.BlockSpec((1, blk, RET_QK_DIM), head),
                  pl.BlockSpec((1, 1, RET_V_DIM), head)],
        out_specs=pl.BlockSpec((blk, RET_V_DIM), row),
        out_shape=jax.ShapeDtypeStruct((t, RET_VW), BF16),
        scratch_shapes=[pltpu.VMEM((RET_QK_DIM, RET_V_DIM), F32)],
        compiler_params=_params(3),
        name="retention",
    )(rq, rk, rv, rg, dm, qd, kd, cd)


def _attn_kernel(lq1_ref, lk1_ref, lq2_ref, lk2_ref, nw_ref, q_ref, k_ref, v_ref, o_ref,
                 m_ref, l_ref, acc_ref, *, blk, lambda_init):
    qi = pl.program_id(2)
    m_ref[...] = jnp.full_like(m_ref, -jnp.inf)
    l_ref[...] = jnp.zeros_like(l_ref)
    acc_ref[...] = jnp.zeros_like(acc_ref)
    q = q_ref[...]

    def block(first_key, width, masked):
        off = pl.multiple_of(first_key, width)
        kb = k_ref[pl.ds(off, width), :]
        vb = v_ref[pl.ds(off, width), :]
        if masked:
            rchunk = lax.broadcasted_iota(jnp.int32, (blk, width), 0) // CHUNK
            cchunk = lax.broadcasted_iota(jnp.int32, (blk, width), 1) // CHUNK
            visible = cchunk <= rchunk
        for c in range(2):
            qc = q[:, c * DIFF_QK_DIM:(c + 1) * DIFF_QK_DIM]
            kc = kb[:, c * DIFF_QK_DIM:(c + 1) * DIFF_QK_DIM]
            s = lax.dot_general(qc, kc, (((1,), (1,)), ((), ())), preferred_element_type=F32)
            if masked:
                s = jnp.where(visible, s, -jnp.inf)
            m_prev = m_ref[c]
            m_new = jnp.maximum(m_prev, jnp.max(s, axis=-1, keepdims=True))
            p = jnp.exp2(s - jnp.tile(m_new, (1, width // LANES)))
            alpha = jnp.exp2(m_prev - m_new)
            l_ref[c] = alpha * l_ref[c] + jnp.sum(p, axis=-1, keepdims=True)
            acc_ref[c] = (jnp.tile(alpha, (1, DIFF_V_DIM // LANES)) * acc_ref[c]
                          + jnp.dot(p.astype(BF16), vb, preferred_element_type=F32))
            m_ref[c] = m_new

    def body(j, carry):
        block(j * (2 * blk), 2 * blk, False)
        return carry

    lax.fori_loop(0, qi // 2, body, 0)

    @pl.when(lax.rem(qi, 2) == 1)
    def _():
        block((qi - 1) * blk, blk, False)

    block(qi * blk, blk, True)

    lam = (jnp.exp(jnp.sum(lq1_ref[...] * lk1_ref[...], axis=-1, keepdims=True))
           - jnp.exp(jnp.sum(lq2_ref[...] * lk2_ref[...], axis=-1, keepdims=True))
           + lambda_init)
    rep = DIFF_V_DIM // LANES
    o = (acc_ref[0] / jnp.tile(l_ref[0], (1, rep))
         - lam * (acc_ref[1] / jnp.tile(l_ref[1], (1, rep))))
    ms = jnp.mean(o * o, axis=-1, keepdims=True)
    o_ref[...] = (o * lax.rsqrt(ms + EPS) * nw_ref[...] * (1.0 - lambda_init)).astype(o_ref.dtype)


def _diff_attention(dq, dk, dv, lq1, lk1, lq2, lk2, norm_w, batch, seq, lambda_init):
    t = batch * seq
    blk = min(ATT_BLOCK, seq)
    nq = seq // blk
    hw = 2 * DIFF_QK_DIM
    vec = lambda b, h, i: (0, 0)
    return pl.pallas_call(
        functools.partial(_attn_kernel, blk=blk, lambda_init=lambda_init),
        grid=(batch, DIFF_HEADS, nq),
        in_specs=[pl.BlockSpec((1, DIFF_QK_DIM), vec)] * 4
        + [pl.BlockSpec((1, DIFF_V_DIM), vec),
           pl.BlockSpec((blk, hw), lambda b, h, i: (b * nq + i, h)),
           pl.BlockSpec((seq, hw), lambda b, h, i: (b, h)),
           pl.BlockSpec((seq, DIFF_V_DIM), lambda b, h, i: (b, h))],
        out_specs=pl.BlockSpec((blk, DIFF_V_DIM), lambda b, h, i: (b * nq + i, h)),
        out_shape=jax.ShapeDtypeStruct((t, DIFF_VW), BF16),
        scratch_shapes=[pltpu.VMEM((2, blk, LANES), F32),
                        pltpu.VMEM((2, blk, LANES), F32),
                        pltpu.VMEM((2, blk, DIFF_V_DIM), F32)],
        compiler_params=_params(3),
        name="diff_attention",
    )(lq1, lk1, lq2, lk2, norm_w, dq, dk, dv)


def _merge_kernel(r_ref, d_ref, wr_ref, wd_ref, gr_ref, gd_ref, o_ref):
    a = jnp.dot(r_ref[...], wr_ref[...], preferred_element_type=F32)
    b = jnp.dot(d_ref[...], wd_ref[...], preferred_element_type=F32)
    o_ref[...] = (gr_ref[...].astype(F32) * a + gd_ref[...].astype(F32) * b).astype(o_ref.dtype)


def _merge(ret, da, wr, wd, gr, gd):
    t, kdim = ret.shape
    n = wr.shape[1]
    tm, tn = min(MERGE_TM, t), MERGE_TN
    lhs = pl.BlockSpec((tm, kdim), lambda i, j: (i, 0))
    rhs = pl.BlockSpec((kdim, tn), lambda i, j: (0, j))
    tile = pl.BlockSpec((tm, tn), lambda i, j: (i, j))
    return pl.pallas_call(
        _merge_kernel,
        grid=(t // tm, n // tn),
        in_specs=[lhs, lhs, rhs, rhs, tile, tile],
        out_specs=tile,
        out_shape=jax.ShapeDtypeStruct((t, n), BF16),
        compiler_params=_params(2),
        name="merge",
    )(ret, da, wr, wd, gr, gd)


def _outproj_kernel(m_ref, w_ref, x_ref, o_ref):
    o_ref[...] = DN_ALPHA * x_ref[...] + jnp.dot(m_ref[...], w_ref[...],
                                                  preferred_element_type=F32)


def _outproj(merged, w, x):
    t, kdim = merged.shape
    n = w.shape[1]
    tm, tn = min(PROJ_TM, t), PROJ_TN
    tile = pl.BlockSpec((tm, tn), lambda i, j: (i, j))
    return pl.pallas_call(
        _outproj_kernel,
        grid=(t // tm, n // tn),
        in_specs=[pl.BlockSpec((tm, kdim), lambda i, j: (i, 0)),
                  pl.BlockSpec((kdim, tn), lambda i, j: (0, j)),
                  tile],
        out_specs=tile,
        out_shape=jax.ShapeDtypeStruct((t, n), F32),
        compiler_params=_params(2),
        name="outproj",
    )(merged, w, x)


def _layernorm(y, w, b):
    mu = jnp.mean(y, axis=-1, keepdims=True)
    yc = y - mu
    var = jnp.mean(yc * yc, axis=-1, keepdims=True)
    return yc * lax.rsqrt(var + EPS) * w + b


def _ln_route_kernel(y_ref, lw_ref, lb_ref, wr_ref, rb_ref, h_ref, te_ref, tw_ref, tr_ref, tc_ref,
                     cnt_ref):
    h = _layernorm(y_ref[...], lw_ref[...], lb_ref[...])
    h_ref[...] = h
    hi = h.astype(BF16)
    lo = (h - hi.astype(F32)).astype(BF16)
    w = wr_ref[...]
    whi = w.astype(BF16)
    wlo = (w - whi.astype(F32)).astype(BF16)
    logits = (jnp.dot(hi, whi, preferred_element_type=F32)
              + (jnp.dot(hi, wlo, preferred_element_type=F32)
                 + jnp.dot(lo, whi, preferred_element_type=F32)))
    scores = _sigmoid(logits)
    biased = scores + rb_ref[...]
    tm = scores.shape[0]
    neg = -jnp.inf
    lane_i = lax.broadcasted_iota(jnp.int32, (tm, N_EXPERTS), 1)
    lane = lane_i.astype(F32)
    per_group = N_EXPERTS // N_GROUPS
    grp = lane_i // per_group

    gscores = []
    gs_lane = jnp.zeros((tm, N_EXPERTS), F32)
    for g in range(N_GROUPS):
        mk = grp == g
        vals = jnp.where(mk, biased, neg)
        m1 = jnp.max(vals, axis=-1, keepdims=True)
        i1 = jnp.min(jnp.where(vals == m1, lane, float(N_EXPERTS)), axis=-1, keepdims=True)
        m2 = jnp.max(jnp.where(lane == i1, neg, vals), axis=-1, keepdims=True)
        gscores.append(m1 + m2)
        gs_lane = jnp.where(mk, m1 + m2, gs_lane)
    beaten = jnp.zeros((tm, N_EXPERTS), F32)
    for g in range(N_GROUPS):
        wins = jnp.where(gscores[g] > gs_lane, 1.0,
                         jnp.where(gscores[g] == gs_lane,
                                   jnp.where(grp > g, 1.0, 0.0), 0.0))
        beaten = beaten + wins
    cur = jnp.where(beaten < float(TOPK_GROUPS), biased, neg)

    out_lane = lax.broadcasted_iota(jnp.int32, (tm, LANES), 1)
    sel_e = jnp.zeros((tm, LANES), F32)
    sel_w = jnp.zeros((tm, LANES), F32)
    wsum = jnp.zeros((tm, 1), F32)
    hits = []
    for k in range(TOP_K):
        m = jnp.max(cur, axis=-1, keepdims=True)
        idx = jnp.min(jnp.where(cur == m, lane, float(N_EXPERTS)), axis=-1, keepdims=True)
        hit = lane == idx
        wk = jnp.sum(jnp.where(hit, scores, 0.0), axis=-1, keepdims=True)
        cur = jnp.where(hit, neg, cur)
        wsum = wsum + wk
        sel_e = jnp.where(out_lane == k, idx, sel_e)
        sel_w = jnp.where(out_lane == k, wk, sel_w)
        hits.append(hit)
    te_ref[...] = sel_e.astype(jnp.int32)
    tw_ref[...] = sel_w / wsum * ROUTED_SCALE

    @pl.when(pl.program_id(0) == 0)
    def _():
        cnt_ref[...] = jnp.zeros_like(cnt_ref)

    chosen = jnp.zeros((tm, N_EXPERTS), F32)
    for hit in hits:
        chosen = chosen + jnp.where(hit, 1.0, 0.0)
    earlier = (lax.broadcasted_iota(jnp.int32, (tm, tm), 0)
               > lax.broadcasted_iota(jnp.int32, (tm, tm), 1))
    prefix = jnp.dot(jnp.where(earlier, 1.0, 0.0).astype(BF16), chosen.astype(BF16),
                     preferred_element_type=F32) + cnt_ref[...]
    sel_r = jnp.zeros((tm, LANES), F32)
    for k, hit in enumerate(hits):
        rk = jnp.sum(jnp.where(hit, prefix, 0.0), axis=-1, keepdims=True)
        sel_r = jnp.where(out_lane == k, rk, sel_r)
    tr_ref[...] = sel_r.astype(jnp.int32)
    cnt_ref[...] = cnt_ref[...] + jnp.sum(chosen, axis=0, keepdims=True)
    tc_ref[...] = cnt_ref[...]


def _ln_route(y, ln_w, ln_b, w_router, router_bias):
    t, d = y.shape
    tm = min(LN_TM, t)
    rowt = pl.BlockSpec((tm, d), lambda i: (i, 0))
    vec = pl.BlockSpec((1, d), lambda i: (0, 0))
    narrow = pl.BlockSpec((tm, LANES), lambda i: (i, 0))
    return pl.pallas_call(
        _ln_route_kernel,
        grid=(t // tm,),
        in_specs=[rowt, vec, vec,
                  pl.BlockSpec((d, N_EXPERTS), lambda i: (0, 0)),
                  pl.BlockSpec((1, N_EXPERTS), lambda i: (0, 0))],
        out_specs=[rowt, narrow, narrow, narrow,
                   pl.BlockSpec((1, N_EXPERTS), lambda i: (0, 0))],
        out_shape=[jax.ShapeDtypeStruct((t, d), F32),
                   jax.ShapeDtypeStruct((t, LANES), jnp.int32),
                   jax.ShapeDtypeStruct((t, LANES), F32),
                   jax.ShapeDtypeStruct((t, LANES), jnp.int32),
                   jax.ShapeDtypeStruct((1, N_EXPERTS), F32)],
        scratch_shapes=[pltpu.VMEM((1, N_EXPERTS), F32)],
        compiler_params=_params(1),
        name="ln_route",
    )(y, ln_w, ln_b, w_router, router_bias)


def _expert_plan(top_e, rank, counts, t):
    nb = t * TOP_K // MOE_BLOCK + N_EXPERTS
    counts = counts.reshape(N_EXPERTS).astype(jnp.int32)
    padded = (counts + MOE_BLOCK - 1) // MOE_BLOCK * MOE_BLOCK
    pad_end = jnp.cumsum(padded)
    pad_start = pad_end - padded
    experts = jnp.arange(N_EXPERTS, dtype=jnp.int32)
    dest = jnp.sum(jnp.where(top_e[..., None] == experts, pad_start, 0), axis=-1) + rank
    n_active = (pad_end[-1] // MOE_BLOCK).astype(jnp.int32)
    blk_start = jnp.arange(nb, dtype=jnp.int32) * MOE_BLOCK
    block_e = jnp.sum((pad_end[None, :] <= blk_start[:, None]).astype(jnp.int32), axis=-1)
    block_e = jnp.minimum(block_e, N_EXPERTS - 1)
    last_e = jnp.sum((pad_end <= (n_active - 1) * MOE_BLOCK).astype(jnp.int32))
    block_e = jnp.where(jnp.arange(nb) < n_active, block_e, jnp.minimum(last_e, N_EXPERTS - 1))
    tail_pieces = (nb * MOE_BLOCK - pad_end[-1]) // (MOE_BLOCK // 2)
    zero_start = jnp.concatenate([pad_start + counts, pad_end[-1:]]).astype(jnp.int32)
    zero_cnt = jnp.concatenate([padded - counts, tail_pieces[None]]).astype(jnp.int32)
    return (dest.astype(jnp.int32), block_e.astype(jnp.int32), n_active.reshape(1),
            zero_start, zero_cnt)


def _pack_halves(v):
    half = v.shape[-1] // 2
    bits = lambda a: lax.bitcast_convert_type(a.astype(BF16).astype(F32), jnp.uint32)
    return (bits(v[:, :half]) >> 16) | (bits(v[:, half:]) & jnp.uint32(0xFFFF0000))


def _unpack_halves(w):
    lo = lax.bitcast_convert_type(w << 16, F32)
    hi = lax.bitcast_convert_type(w & jnp.uint32(0xFFFF0000), F32)
    return lo, hi


def _dispatch_kernel(zs_ref, zn_ref, dest_ref, h_ref, xs_hbm, buf, zbuf, sem, zsem, *, tm):
    i = pl.program_id(0)
    n = pl.num_programs(0)
    slot = lax.rem(i, 2)

    def zero_rows(wait):
        def zero_copy(off, rows):
            cp = pltpu.make_async_copy(zbuf.at[pl.ds(0, rows)], xs_hbm.at[pl.ds(off, rows)], zsem)
            if wait:
                cp.wait()
            else:
                cp.start()

        def single_rows(off, count):
            def one(j, carry):
                zero_copy(off + j, 1)
                return carry
            lax.fori_loop(0, count, one, 0)

        def body(e, carry):
            cnt = zn_ref[e]
            start = zs_ref[e]
            head = jnp.minimum((-start) & (SUBLANES - 1), cnt)
            single_rows(start, head)
            rest = cnt - head
            off = start + head
            rows = MOE_BLOCK // 2
            while rows >= SUBLANES:
                @pl.when((rest & rows) != 0)
                def _():
                    zero_copy(pl.multiple_of(off, SUBLANES), rows)
                off = off + (rest & rows)
                rows //= 2
            single_rows(off, rest & (SUBLANES - 1))
            return carry
        lax.fori_loop(0, N_EXPERTS, body, 0)

        def tail(c, carry):
            zero_copy(pl.multiple_of(zs_ref[N_EXPERTS] + c * (MOE_BLOCK // 2), MOE_BLOCK // 2),
                      MOE_BLOCK // 2)
            return carry
        lax.fori_loop(0, zn_ref[N_EXPERTS], tail, 0)

    @pl.when(i == 0)
    def _():
        zbuf[...] = jnp.zeros_like(zbuf)
        zero_rows(False)
        zero_rows(True)

    def wait_rows(s):
        for _ in range(TOP_K):
            pltpu.make_async_copy(buf.at[s], xs_hbm.at[pl.ds(0, tm)], sem.at[s]).wait()

    @pl.when(i >= 2)
    def _():
        wait_rows(slot)

    buf[slot] = _pack_halves(h_ref[...])

    def body(r, carry):
        for k in range(TOP_K):
            pltpu.make_async_copy(buf.at[slot, pl.ds(r, 1)],
                                  xs_hbm.at[pl.ds(dest_ref[0, 0, r * TOP_K + k], 1)],
                                  sem.at[slot]).start()
        return carry
    lax.fori_loop(0, tm, body, 0)

    @pl.when(i == n - 1)
    def _():
        wait_rows(slot)

    @pl.when((i == n - 1) & (n >= 2))
    def _():
        wait_rows(1 - slot)


def _dispatch(h, dest, zero_start, zero_cnt, nb):
    t, d = h.shape
    tm = min(DISPATCH_TM, t)
    grid_spec = pltpu.PrefetchScalarGridSpec(
        num_scalar_prefetch=2,
        grid=(t // tm,),
        in_specs=[pl.BlockSpec((1, 1, tm * TOP_K), lambda i, zs, zn: (i, 0, 0),
                               memory_space=pltpu.SMEM),
                  pl.BlockSpec((tm, d), lambda i, zs, zn: (i, 0))],
        out_specs=pl.BlockSpec(memory_space=pl.ANY),
        scratch_shapes=[pltpu.VMEM((2, tm, d // 2), jnp.uint32),
                        pltpu.VMEM((MOE_BLOCK // 2, d // 2), jnp.uint32),
                        pltpu.SemaphoreType.DMA((2,)),
                        pltpu.SemaphoreType.DMA(())],
    )
    return pl.pallas_call(
        functools.partial(_dispatch_kernel, tm=tm),
        grid_spec=grid_spec,
        out_shape=jax.ShapeDtypeStruct((nb * MOE_BLOCK, d // 2), jnp.uint32),
        compiler_params=_params(1),
        name="dispatch",
    )(zero_start, zero_cnt, dest.reshape(t // tm, 1, tm * TOP_K), h)


def _expert_kernel(be_ref, na_ref, x_ref, wg_ref, wu_ref, wd_ref, o_ref):
    i = pl.program_id(0)

    @pl.when(i < na_ref[0])
    def _():
        lo, hi = _unpack_halves(x_ref[...])
        lo, hi = lo.astype(BF16), hi.astype(BF16)
        half = lo.shape[-1]

        def in_proj(w_ref):
            return (jnp.dot(lo, w_ref[:half, :], preferred_element_type=F32)
                    + jnp.dot(hi, w_ref[half:, :], preferred_element_type=F32))

        g = in_proj(wg_ref)
        u = in_proj(wu_ref)
        act = (g * _sigmoid(g) * u).astype(BF16)
        o_ref[...] = _pack_halves(jnp.dot(act, wd_ref[...], preferred_element_type=F32))

    @pl.when(i >= na_ref[0])
    def _():
        o_ref[...] = jnp.zeros_like(o_ref)


def _routed_experts(xs, block_e, n_active, wg, wu, wd):
    p, dh = xs.shape
    nb = p // MOE_BLOCK
    d, e_dim = wg.shape[-2:]
    expert = lambda i, be, na: (be[i], 0, 0)
    grid_spec = pltpu.PrefetchScalarGridSpec(
        num_scalar_prefetch=2,
        grid=(nb,),
        in_specs=[pl.BlockSpec((MOE_BLOCK, dh),
                               lambda i, be, na: (jnp.minimum(i, jnp.maximum(na[0] - 1, 0)), 0)),
                  pl.BlockSpec((None, d, e_dim), expert),
                  pl.BlockSpec((None, d, e_dim), expert),
                  pl.BlockSpec((None, e_dim, d), expert)],
        out_specs=pl.BlockSpec((MOE_BLOCK, dh), lambda i, be, na: (i, 0)),
    )
    return pl.pallas_call(
        _expert_kernel,
        grid_spec=grid_spec,
        out_shape=jax.ShapeDtypeStruct((p, dh), jnp.uint32),
        compiler_params=_params(1),
        name="routed_experts",
    )(block_e, n_active, xs, wg, wu, wd)


def _shared_kernel(h_ref, wg_ref, wu_ref, wd_ref, o_ref):
    x = h_ref[...].astype(BF16)
    g = jnp.dot(x, wg_ref[...], preferred_element_type=F32)
    u = jnp.dot(x, wu_ref[...], preferred_element_type=F32)
    act = (g * _sigmoid(g) * u).astype(BF16)
    o_ref[...] = jnp.dot(act, wd_ref[...], preferred_element_type=F32)


def _shared_expert(h, wg, wu, wd):
    t, d = h.shape
    e_dim = wg.shape[-1]
    tm = min(SHARED_TM, t)
    rowt = pl.BlockSpec((tm, d), lambda i: (i, 0))
    return pl.pallas_call(
        _shared_kernel,
        grid=(t // tm,),
        in_specs=[rowt,
                  pl.BlockSpec((d, e_dim), lambda i: (0, 0)),
                  pl.BlockSpec((d, e_dim), lambda i: (0, 0)),
                  pl.BlockSpec((e_dim, d), lambda i: (0, 0))],
        out_specs=rowt,
        out_shape=jax.ShapeDtypeStruct((t, d), F32),
        compiler_params=_params(1),
        name="shared_expert",
    )(h, wg, wu, wd)


def _final_kernel(dest_ref, destn_ref, ys_hbm, h_ref, s_ref, tw_ref, lw_ref, lb_ref, o_ref,
                  gbuf, sem, *, tm):
    i = pl.program_id(0)
    n = pl.num_programs(0)
    slot = lax.rem(i, 2)

    def gather_start(idx_ref, s):
        def body(r, carry):
            for k in range(TOP_K):
                pltpu.make_async_copy(ys_hbm.at[pl.ds(idx_ref[0, 0, r * TOP_K + k], 1)],
                                      gbuf.at[s, k, pl.ds(r, 1)], sem.at[s]).start()
            return carry
        lax.fori_loop(0, tm, body, 0)

    @pl.when(i == 0)
    def _():
        gather_start(dest_ref, 0)

    @pl.when(i + 1 < n)
    def _():
        gather_start(destn_ref, 1 - slot)

    for k in range(TOP_K):
        pltpu.make_async_copy(ys_hbm.at[pl.ds(0, tm)], gbuf.at[slot, k], sem.at[slot]).wait()

    tw = tw_ref[...]
    half = gbuf.shape[-1]
    y_lo = DN_ALPHA * h_ref[:, :half] + s_ref[:, :half]
    y_hi = DN_ALPHA * h_ref[:, half:] + s_ref[:, half:]
    for k in range(TOP_K):
        lo, hi = _unpack_halves(gbuf[slot, k])
        y_lo = y_lo + tw[:, k:k + 1] * lo
        y_hi = y_hi + tw[:, k:k + 1] * hi
    inv_d = 1.0 / (2 * half)
    mu = (jnp.sum(y_lo, axis=-1, keepdims=True) + jnp.sum(y_hi, axis=-1, keepdims=True)) * inv_d
    c_lo, c_hi = y_lo - mu, y_hi - mu
    var = (jnp.sum(c_lo * c_lo, axis=-1, keepdims=True)
           + jnp.sum(c_hi * c_hi, axis=-1, keepdims=True)) * inv_d
    r = lax.rsqrt(var + EPS)
    o_ref[:, :half] = c_lo * r * lw_ref[:, :half] + lb_ref[:, :half]
    o_ref[:, half:] = c_hi * r * lw_ref[:, half:] + lb_ref[:, half:]


def _final(ys, dest, h, shared, top_w, ln_w, ln_b):
    t, d = h.shape
    tm = min(FINAL_TM, t)
    n = t // tm
    idx = dest.reshape(n, 1, tm * TOP_K)
    idx_spec = lambda fn: pl.BlockSpec((1, 1, tm * TOP_K), fn, memory_space=pltpu.SMEM)
    rowt = pl.BlockSpec((tm, d), lambda i: (i, 0))
    vec = pl.BlockSpec((1, d), lambda i: (0, 0))
    return pl.pallas_call(
        functools.partial(_final_kernel, tm=tm),
        grid=(n,),
        in_specs=[idx_spec(lambda i: (i, 0, 0)),
                  idx_spec(lambda i: (jnp.minimum(i + 1, n - 1), 0, 0)),
                  pl.BlockSpec(memory_space=pl.ANY),
                  rowt, rowt,
                  pl.BlockSpec((tm, LANES), lambda i: (i, 0)),
                  vec, vec],
        out_specs=rowt,
        out_shape=jax.ShapeDtypeStruct((t, d), F32),
        scratch_shapes=[pltpu.VMEM((2, TOP_K, tm, d // 2), jnp.uint32),
                        pltpu.SemaphoreType.DMA((2,))],
        compiler_params=_params(1),
        name="combine_ln2",
    )(idx, idx, ys, h, shared, top_w, ln_w, ln_b)


def _rotary_tables(positions):
    pos = positions.reshape(-1).astype(F32)[:, None]
    ret_freqs = 1.0 / (RET_THETA ** jnp.linspace(0.0, 1.0, RET_QK_DIM // 2, dtype=F32))
    ang = pos * ret_freqs
    ret_rot = (jnp.cos(ang), jnp.sin(ang))
    rope_freqs = ROPE_THETA ** (-jnp.arange(0, DIFF_QK_DIM, 2, dtype=F32) / DIFF_QK_DIM)
    ang = pos * rope_freqs
    cos, sin = jnp.cos(ang), jnp.sin(ang)
    diff_rot = (jnp.concatenate([cos, cos], axis=-1), jnp.concatenate([-sin, sin], axis=-1))
    return ret_rot, diff_rot


def _layer(h, positions, w_in, w_ret_proj, w_diff_proj, w_out, lq1, lk1, lq2, lk2, diff_norm_w,
           ln1_w, ln1_b, w_router, router_bias, exp_gate, exp_up, exp_down,
           shared_gate, shared_up, shared_down, ln2_w, ln2_b, lambda_init):
    batch, seq, d = h.shape
    t = batch * seq
    x = h.reshape(t, d)
    xb = x.astype(BF16)
    w_in = w_in.astype(BF16)
    ret_rot, diff_rot = _rotary_tables(positions)

    off = 0
    rq = _proj(xb, w_in, off, RET_QW, "ret_rot", 1.0, ret_rot); off += RET_QW
    rk = _proj(xb, w_in, off, RET_QW, "ret_rot", RET_QK_DIM ** -0.5, ret_rot); off += RET_QW
    rv = _proj(xb, w_in, off, RET_VW, "plain"); off += RET_VW
    rg = _proj(xb, w_in, off, RET_VW, "silu"); off += RET_VW
    dq = _proj(xb, w_in, off, DIFF_QW, "diff_rot", LOG2_E * DIFF_QK_DIM ** -0.5, diff_rot)
    off += DIFF_QW
    dk = _proj(xb, w_in, off, DIFF_QW, "diff_rot", 1.0, diff_rot); off += DIFF_QW
    dv = _proj(xb, w_in, off, DIFF_VW, "plain"); off += DIFF_VW
    gate_ret = _proj(xb, w_in, off, d, "sigmoid"); off += d
    gate_diff = _proj(xb, w_in, off, d, "sigmoid")

    ret = _retention(rq, rk, rv, rg, batch, seq)
    row = lambda v: v.reshape(1, -1).astype(F32)
    da = _diff_attention(dq, dk, dv, row(lq1), row(lk1), row(lq2), row(lk2), row(diff_norm_w),
                         batch, seq, lambda_init)
    merged = _merge(ret, da, w_ret_proj.astype(BF16), w_diff_proj.astype(BF16),
                    gate_ret, gate_diff)
    y1 = _outproj(merged, w_out.astype(BF16), x)
    h1, top_e, top_w, rank, counts = _ln_route(y1, row(ln1_w), row(ln1_b), w_router,
                                               row(router_bias))

    dest, block_e, n_active, zero_start, zero_cnt = _expert_plan(
        top_e[:, :TOP_K], rank[:, :TOP_K], counts, t)
    xs = _dispatch(h1, dest, zero_start, zero_cnt, block_e.shape[0])
    ys = _routed_experts(xs, block_e, n_active, exp_gate.astype(BF16), exp_up.astype(BF16),
                         exp_down.astype(BF16))
    shared = _shared_expert(h1, shared_gate.astype(BF16), shared_up.astype(BF16),
                            shared_down.astype(BF16))
    out = _final(ys, dest, h1, shared, top_w, row(ln2_w), row(ln2_b))
    return out.reshape(batch, seq, d)


def kernel(x, positions, w_in, w_ret_proj, w_diff_proj, w_out, lambda_q1, lambda_k1, lambda_q2,
           lambda_k2, diff_norm_w, ln1_w, ln1_b, w_router, router_bias, exp_gate, exp_up,
           exp_down, shared_gate, shared_up, shared_down, ln2_w, ln2_b):
    h = x
    for l in range(w_in.shape[0]):
        lambda_init = 0.8 - 0.6 * math.exp(-0.3 * l)
        h = _layer(h, positions, w_in[l], w_ret_proj[l], w_diff_proj[l], w_out[l],
                   lambda_q1[l], lambda_k1[l], lambda_q2[l], lambda_k2[l], diff_norm_w[l],
                   ln1_w[l], ln1_b[l], w_router[l], router_bias[l], exp_gate[l], exp_up[l],
                   exp_down[l], shared_gate[l], shared_up[l], shared_down[l],
                   ln2_w[l], ln2_b[l], lambda_init)
    return h
```

```python
import functools
import math

import jax
import jax.numpy as jnp
from jax import lax
from jax.experimental import pallas as pl
from jax.experimental.pallas import tpu as pltpu

F32 = jnp.float32
BF16 = jnp.bfloat16

D_MODEL = 4096
CHUNK = 64
RET_HEADS = 8
RET_QK_DIM = 256
RET_V_DIM = 512
RET_THETA = 10000.0
DIFF_HEADS = 16
DIFF_QK_DIM = 128
DIFF_V_DIM = 256
ROPE_THETA = 10000.0
RET_QW = RET_HEADS * RET_QK_DIM
RET_VW = RET_HEADS * RET_V_DIM
DIFF_QW = DIFF_HEADS * 2 * DIFF_QK_DIM
DIFF_VW = DIFF_HEADS * DIFF_V_DIM
N_EXPERTS = 64
TOP_K = 8
N_GROUPS = 8
TOPK_GROUPS = 4
EXPERT_DIM = 512
ROUTED_SCALE = 2.5
DEPTH = 1
DN_ALPHA = (2.0 * DEPTH) ** 0.25
EPS = 1e-5
LOG2_E = math.log2(math.e)

LANES = 128
SUBLANES = 8
MIB = 1024 * 1024
VMEM_LIMIT = 56 * MIB
EXPERT_VMEM_LIMIT = 60 * MIB

PROJ_TM, PROJ_TN = 1024, 512
MERGE_TM, MERGE_TN = 512, 512
RET_BLOCK = 256
ATT_BLOCK = 512
LN_TM = 256
MOE_BLOCK = 256
DISPATCH_TM = 128
SHARED_TM = 256
FINAL_TM = 128


def _params(n_axes, vmem_limit=VMEM_LIMIT):
    return pltpu.CompilerParams(dimension_semantics=("arbitrary",) * n_axes,
                                vmem_limit_bytes=vmem_limit)


def _sigmoid(v):
    return 1.0 / (1.0 + jnp.exp(-v))


def _proj_kernel(*refs, mode, scale, tn):
    if mode in ("ret_rot", "diff_rot"):
        x_ref, w_ref, c_ref, s_ref, o_ref = refs
    else:
        x_ref, w_ref, o_ref = refs
    acc = jnp.dot(x_ref[...], w_ref[...].astype(BF16), preferred_element_type=F32)
    if mode == "ret_rot":
        cos, sin = c_ref[...], s_ref[...]
        for h in range(tn // RET_QK_DIM):
            lo = h * RET_QK_DIM
            x1 = acc[:, lo:lo + LANES]
            x2 = acc[:, lo + LANES:lo + 2 * LANES]
            o_ref[:, lo:lo + LANES] = ((x1 * cos - x2 * sin) * scale).astype(o_ref.dtype)
            o_ref[:, lo + LANES:lo + 2 * LANES] = ((x2 * cos + x1 * sin) * scale).astype(o_ref.dtype)
    elif mode == "diff_rot":
        c, s = c_ref[...], s_ref[...]
        for g in range(tn // LANES):
            xg = acc[:, g * LANES:(g + 1) * LANES]
            rot = pltpu.roll(xg, LANES // 2, axis=1)
            o_ref[:, g * LANES:(g + 1) * LANES] = ((xg * c + rot * s) * scale).astype(o_ref.dtype)
    elif mode == "silu":
        o_ref[...] = (acc * _sigmoid(acc)).astype(o_ref.dtype)
    elif mode == "sigmoid":
        o_ref[...] = _sigmoid(acc).astype(o_ref.dtype)
    else:
        o_ref[...] = acc.astype(o_ref.dtype)


def _proj(xb, w, col_off, width, mode, scale=1.0, rot=None):
    t, d = xb.shape
    tm, tn = min(PROJ_TM, t), PROJ_TN
    joff = col_off // tn
    in_specs = [pl.BlockSpec((tm, d), lambda i, j: (i, 0)),
                pl.BlockSpec((d, tn), lambda i, j: (0, j + joff))]
    args = [xb, w]
    if rot is not None:
        in_specs += [pl.BlockSpec((tm, LANES), lambda i, j: (i, 0))] * 2
        args += list(rot)
    return pl.pallas_call(
        functools.partial(_proj_kernel, mode=mode, scale=scale, tn=tn),
        grid=(t // tm, width // tn),
        in_specs=in_specs,
        out_specs=pl.BlockSpec((tm, tn), lambda i, j: (i, j)),
        out_shape=jax.ShapeDtypeStruct((t, width), BF16),
        compiler_params=_params(2),
        name="proj_" + mode,
    )(*args)


def _ret_kernel(q_ref, k_ref, v_ref, g_ref, dm_ref, qd_ref, kd_ref, cd_ref, o_ref, r_ref):
    @pl.when(pl.program_id(2) == 0)
    def _():
        r_ref[...] = jnp.zeros_like(r_ref)

    q, k, v = q_ref[...], k_ref[...], v_ref[...]
    s = lax.dot_general(q, k, (((1,), (1,)), ((), ())), preferred_element_type=F32)
    s = (s * dm_ref[0]).astype(BF16)
    inner = jnp.dot(s, v, preferred_element_type=F32)
    qs = (q.astype(F32) * qd_ref[0]).astype(BF16)
    r = r_ref[...]
    cross = jnp.dot(qs, r.astype(BF16), preferred_element_type=F32)
    ks = (k.astype(F32) * kd_ref[0]).astype(BF16)
    r_ref[...] = r * cd_ref[0] + lax.dot_general(
        ks, v, (((0,), (0,)), ((), ())), preferred_element_type=F32)
    y = inner + cross
    mu = jnp.mean(y, axis=-1, keepdims=True)
    yc = y - mu
    var = jnp.mean(yc * yc, axis=-1, keepdims=True)
    o_ref[...] = (yc * lax.rsqrt(var + EPS) * g_ref[...].astype(F32)).astype(o_ref.dtype)


def _retention_tables(blk):
    log_g = jnp.log1p(-jnp.exp2(-5.0 - jnp.arange(RET_HEADS, dtype=F32)))
    idx = jnp.arange(blk, dtype=F32)
    chunk = jnp.arange(blk) // CHUNK
    visible = chunk[None, :] <= chunk[:, None]
    dm = jnp.where(visible[None],
                   jnp.exp(log_g[:, None, None] * jnp.abs(idx[:, None] - idx[None, :])), 0.0)
    qd = jnp.exp(log_g[:, None] * (idx + 1.0))
    kd = jnp.exp(log_g[:, None] * (blk - 1.0 - idx))
    cd = jnp.exp(log_g * blk)
    qd = jnp.broadcast_to(qd[:, :, None], (RET_HEADS, blk, RET_QK_DIM))
    kd = jnp.broadcast_to(kd[:, :, None], (RET_HEADS, blk, RET_QK_DIM))
    cd = jnp.broadcast_to(cd[:, None, None], (RET_HEADS, 1, RET_V_DIM))
    return dm, qd, kd, cd


def _retention(rq, rk, rv, rg, batch, seq):
    t = batch * seq
    blk = min(RET_BLOCK, seq)
    nl = seq // blk
    dm, qd, kd, cd = _retention_tables(blk)
    row = lambda b, h, l: (b * nl + l, h)
    head = lambda b, h, l: (h, 0, 0)
    return pl.pallas_call(
        _ret_kernel,
        grid=(batch, RET_HEADS, nl),
        in_specs=[pl.BlockSpec((blk, RET_QK_DIM), row),
                  pl.BlockSpec((blk, RET_QK_DIM), row),
                  pl.BlockSpec((blk, RET_V_DIM), row),
                  pl.BlockSpec((blk, RET_V_DIM), row),
                  pl.BlockSpec((1, blk, blk), head),
                  pl.BlockSpec((1, blk, RET_QK_DIM), head),
                  pl.BlockSpec((1, blk, RET_QK_DIM), head),
                  pl.BlockSpec((1, 1, RET_V_DIM), head)],
        out_specs=pl.BlockSpec((blk, RET_V_DIM), row),
        out_shape=jax.ShapeDtypeStruct((t, RET_VW), BF16),
        scratch_shapes=[pltpu.VMEM((RET_QK_DIM, RET_V_DIM), F32)],
        compiler_params=_params(3),
        name="retention",
    )(rq, rk, rv, rg, dm, qd, kd, cd)


def _attn_kernel(lq1_ref, lk1_ref, lq2_ref, lk2_ref, nw_ref, q_ref, k_ref, v_ref, o_ref,
                 m_ref, l_ref, acc_ref, *, blk, lambda_init):
    qi = pl.program_id(2)
    m_ref[...] = jnp.full_like(m_ref, -jnp.inf)
    l_ref[...] = jnp.zeros_like(l_ref)
    acc_ref[...] = jnp.zeros_like(acc_ref)
    q = q_ref[...]

    def block(first_key, width, masked):
        off = pl.multiple_of(first_key, width)
        kb = k_ref[pl.ds(off, width), :]
        vb = v_ref[pl.ds(off, width), :]
        if masked:
            rchunk = lax.broadcasted_iota(jnp.int32, (blk, width), 0) // CHUNK
            cchunk = lax.broadcasted_iota(jnp.int32, (blk, width), 1) // CHUNK
            visible = cchunk <= rchunk
        for c in range(2):
            qc = q[:, c * DIFF_QK_DIM:(c + 1) * DIFF_QK_DIM]
            kc = kb[:, c * DIFF_QK_DIM:(c + 1) * DIFF_QK_DIM]
            s = lax.dot_general(qc, kc, (((1,), (1,)), ((), ())), preferred_element_type=F32)
            if masked:
                s = jnp.where(visible, s, -jnp.inf)
            m_prev = m_ref[c]
            m_new = jnp.maximum(m_prev, jnp.max(s, axis=-1, keepdims=True))
            p = jnp.exp2(s - jnp.tile(m_new, (1, width // LANES)))
            alpha = jnp.exp2(m_prev - m_new)
            l_ref[c] = alpha * l_ref[c] + jnp.sum(p, axis=-1, keepdims=True)
            acc_ref[c] = (jnp.tile(alpha, (1, DIFF_V_DIM // LANES)) * acc_ref[c]
                          + jnp.dot(p.astype(BF16), vb, preferred_element_type=F32))
            m_ref[c] = m_new

    def body(j, carry):
        block(j * (2 * blk), 2 * blk, False)
        return carry

    lax.fori_loop(0, qi // 2, body, 0)

    @pl.when(lax.rem(qi, 2) == 1)
    def _():
        block((qi - 1) * blk, blk, False)

    block(qi * blk, blk, True)

    lam = (jnp.exp(jnp.sum(lq1_ref[...] * lk1_ref[...], axis=-1, keepdims=True))
           - jnp.exp(jnp.sum(lq2_ref[...] * lk2_ref[...], axis=-1, keepdims=True))
           + lambda_init)
    rep = DIFF_V_DIM // LANES
    o = (acc_ref[0] / jnp.tile(l_ref[0], (1, rep))
         - lam * (acc_ref[1] / jnp.tile(l_ref[1], (1, rep))))
    ms = jnp.mean(o * o, axis=-1, keepdims=True)
    o_ref[...] = (o * lax.rsqrt(ms + EPS) * nw_ref[...] * (1.0 - lambda_init)).astype(o_ref.dtype)


def _diff_attention(dq, dk, dv, lq1, lk1, lq2, lk2, norm_w, batch, seq, lambda_init):
    t = batch * seq
    blk = min(ATT_BLOCK, seq)
    nq = seq // blk
    hw = 2 * DIFF_QK_DIM
    vec = lambda b, h, i: (0, 0)
    return pl.pallas_call(
        functools.partial(_attn_kernel, blk=blk, lambda_init=lambda_init),
        grid=(batch, DIFF_HEADS, nq),
        in_specs=[pl.BlockSpec((1, DIFF_QK_DIM), vec)] * 4
        + [pl.BlockSpec((1, DIFF_V_DIM), vec),
           pl.BlockSpec((blk, hw), lambda b, h, i: (b * nq + i, h)),
           pl.BlockSpec((seq, hw), lambda b, h, i: (b, h)),
           pl.BlockSpec((seq, DIFF_V_DIM), lambda b, h, i: (b, h))],
        out_specs=pl.BlockSpec((blk, DIFF_V_DIM), lambda b, h, i: (b * nq + i, h)),
        out_shape=jax.ShapeDtypeStruct((t, DIFF_VW), BF16),
        scratch_shapes=[pltpu.VMEM((2, blk, LANES), F32),
                        pltpu.VMEM((2, blk, LANES), F32),
                        pltpu.VMEM((2, blk, DIFF_V_DIM), F32)],
        compiler_params=_params(3),
        name="diff_attention",
    )(lq1, lk1, lq2, lk2, norm_w, dq, dk, dv)


def _merge_kernel(r_ref, d_ref, wr_ref, wd_ref, gr_ref, gd_ref, o_ref):
    a = jnp.dot(r_ref[...], wr_ref[...], preferred_element_type=F32)
    b = jnp.dot(d_ref[...], wd_ref[...], preferred_element_type=F32)
    o_ref[...] = (gr_ref[...].astype(F32) * a + gd_ref[...].astype(F32) * b).astype(o_ref.dtype)


def _merge(ret, da, wr, wd, gr, gd):
    t, kdim = ret.shape
    n = wr.shape[1]
    tm, tn = min(MERGE_TM, t), MERGE_TN
    lhs = pl.BlockSpec((tm, kdim), lambda i, j: (i, 0))
    rhs = pl.BlockSpec((kdim, tn), lambda i, j: (0, j))
    tile = pl.BlockSpec((tm, tn), lambda i, j: (i, j))
    return pl.pallas_call(
        _merge_kernel,
        grid=(t // tm, n // tn),
        in_specs=[lhs, lhs, rhs, rhs, tile, tile],
        out_specs=tile,
        out_shape=jax.ShapeDtypeStruct((t, n), BF16),
        compiler_params=_params(2),
        name="merge",
    )(ret, da, wr, wd, gr, gd)


def _outproj_kernel(m_ref, w_ref, x_ref, o_ref):
    o_ref[...] = DN_ALPHA * x_ref[...] + jnp.dot(m_ref[...], w_ref[...].astype(BF16),
                                                  preferred_element_type=F32)


def _outproj(merged, w, x):
    t, kdim = merged.shape
    n = w.shape[1]
    tm, tn = min(PROJ_TM, t), PROJ_TN
    tile = pl.BlockSpec((tm, tn), lambda i, j: (i, j))
    return pl.pallas_call(
        _outproj_kernel,
        grid=(t // tm, n // tn),
        in_specs=[pl.BlockSpec((tm, kdim), lambda i, j: (i, 0)),
                  pl.BlockSpec((kdim, tn), lambda i, j: (0, j)),
                  tile],
        out_specs=tile,
        out_shape=jax.ShapeDtypeStruct((t, n), F32),
        compiler_params=_params(2),
        name="outproj",
    )(merged, w, x)


def _layernorm(y, w, b):
    mu = jnp.mean(y, axis=-1, keepdims=True)
    yc = y - mu
    var = jnp.mean(yc * yc, axis=-1, keepdims=True)
    return yc * lax.rsqrt(var + EPS) * w + b


def _ln_route_kernel(y_ref, lw_ref, lb_ref, wr_ref, rb_ref, h_ref, te_ref, tw_ref, tr_ref, tc_ref,
                     cnt_ref):
    h = _layernorm(y_ref[...], lw_ref[...], lb_ref[...])
    h_ref[...] = h
    hi = h.astype(BF16)
    lo = (h - hi.astype(F32)).astype(BF16)
    w = wr_ref[...]
    whi = w.astype(BF16)
    wlo = (w - whi.astype(F32)).astype(BF16)
    logits = (jnp.dot(hi, whi, preferred_element_type=F32)
              + (jnp.dot(hi, wlo, preferred_element_type=F32)
                 + jnp.dot(lo, whi, preferred_element_type=F32)))
    scores = _sigmoid(logits)
    biased = scores + rb_ref[...]
    tm = scores.shape[0]
    neg = -jnp.inf
    lane_i = lax.broadcasted_iota(jnp.int32, (tm, N_EXPERTS), 1)
    lane = lane_i.astype(F32)
    per_group = N_EXPERTS // N_GROUPS
    grp = lane_i // per_group

    gscores = []
    gs_lane = jnp.zeros((tm, N_EXPERTS), F32)
    for g in range(N_GROUPS):
        mk = grp == g
        vals = jnp.where(mk, biased, neg)
        m1 = jnp.max(vals, axis=-1, keepdims=True)
        i1 = jnp.min(jnp.where(vals == m1, lane, float(N_EXPERTS)), axis=-1, keepdims=True)
        m2 = jnp.max(jnp.where(lane == i1, neg, vals), axis=-1, keepdims=True)
        gscores.append(m1 + m2)
        gs_lane = jnp.where(mk, m1 + m2, gs_lane)
    beaten = jnp.zeros((tm, N_EXPERTS), F32)
    for g in range(N_GROUPS):
        wins = jnp.where(gscores[g] > gs_lane, 1.0,
                         jnp.where(gscores[g] == gs_lane,
                                   jnp.where(grp > g, 1.0, 0.0), 0.0))
        beaten = beaten + wins
    cur = jnp.where(beaten < float(TOPK_GROUPS), biased, neg)

    out_lane = lax.broadcasted_iota(jnp.int32, (tm, LANES), 1)
    sel_e = jnp.zeros((tm, LANES), F32)
    sel_w = jnp.zeros((tm, LANES), F32)
    wsum = jnp.zeros((tm, 1), F32)
    hits = []
    for k in range(TOP_K):
        m = jnp.max(cur, axis=-1, keepdims=True)
        idx = jnp.min(jnp.where(cur == m, lane, float(N_EXPERTS)), axis=-1, keepdims=True)
        hit = lane == idx
        wk = jnp.sum(jnp.where(hit, scores, 0.0), axis=-1, keepdims=True)
        cur = jnp.where(hit, neg, cur)
        wsum = wsum + wk
        sel_e = jnp.where(out_lane == k, idx, sel_e)
        sel_w = jnp.where(out_lane == k, wk, sel_w)
        hits.append(hit)
    te_ref[...] = sel_e.astype(jnp.int32)
    tw_ref[...] = sel_w / wsum * ROUTED_SCALE

    @pl.when(pl.program_id(0) == 0)
    def _():
        cnt_ref[...] = jnp.zeros_like(cnt_ref)

    chosen = jnp.zeros((tm, N_EXPERTS), F32)
    for hit in hits:
        chosen = chosen + jnp.where(hit, 1.0, 0.0)
    earlier = (lax.broadcasted_iota(jnp.int32, (tm, tm), 0)
               > lax.broadcasted_iota(jnp.int32, (tm, tm), 1))
    prefix = jnp.dot(jnp.where(earlier, 1.0, 0.0).astype(BF16), chosen.astype(BF16),
                     preferred_element_type=F32) + cnt_ref[...]
    sel_r = jnp.zeros((tm, LANES), F32)
    for k, hit in enumerate(hits):
        rk = jnp.sum(jnp.where(hit, prefix, 0.0), axis=-1, keepdims=True)
        sel_r = jnp.where(out_lane == k, rk, sel_r)
    tr_ref[...] = sel_r.astype(jnp.int32)
    cnt_ref[...] = cnt_ref[...] + jnp.sum(chosen, axis=0, keepdims=True)
    tc_ref[...] = cnt_ref[...]


def _ln_route(y, ln_w, ln_b, w_router, router_bias):
    t, d = y.shape
    tm = min(LN_TM, t)
    rowt = pl.BlockSpec((tm, d), lambda i: (i, 0))
    vec = pl.BlockSpec((1, d), lambda i: (0, 0))
    narrow = pl.BlockSpec((tm, LANES), lambda i: (i, 0))
    return pl.pallas_call(
        _ln_route_kernel,
        grid=(t // tm,),
        in_specs=[rowt, vec, vec,
                  pl.BlockSpec((d, N_EXPERTS), lambda i: (0, 0)),
                  pl.BlockSpec((1, N_EXPERTS), lambda i: (0, 0))],
        out_specs=[rowt, narrow, narrow, narrow,
                   pl.BlockSpec((1, N_EXPERTS), lambda i: (0, 0))],
        out_shape=[jax.ShapeDtypeStruct((t, d), F32),
                   jax.ShapeDtypeStruct((t, LANES), jnp.int32),
                   jax.ShapeDtypeStruct((t, LANES), F32),
                   jax.ShapeDtypeStruct((t, LANES), jnp.int32),
                   jax.ShapeDtypeStruct((1, N_EXPERTS), F32)],
        scratch_shapes=[pltpu.VMEM((1, N_EXPERTS), F32)],
        compiler_params=_params(1),
        name="ln_route",
    )(y, ln_w, ln_b, w_router, router_bias)


def _expert_plan(top_e, rank, counts, t):
    nb = t * TOP_K // MOE_BLOCK + N_EXPERTS
    counts = counts.reshape(N_EXPERTS).astype(jnp.int32)
    padded = (counts + MOE_BLOCK - 1) // MOE_BLOCK * MOE_BLOCK
    pad_end = jnp.cumsum(padded)
    pad_start = pad_end - padded
    experts = jnp.arange(N_EXPERTS, dtype=jnp.int32)
    dest = jnp.sum(jnp.where(top_e[..., None] == experts, pad_start, 0), axis=-1) + rank
    n_active = (pad_end[-1] // MOE_BLOCK).astype(jnp.int32)
    blk_start = jnp.arange(nb, dtype=jnp.int32) * MOE_BLOCK
    block_e = jnp.sum((pad_end[None, :] <= blk_start[:, None]).astype(jnp.int32), axis=-1)
    block_e = jnp.minimum(block_e, N_EXPERTS - 1)
    last_e = jnp.sum((pad_end <= (n_active - 1) * MOE_BLOCK).astype(jnp.int32))
    block_e = jnp.where(jnp.arange(nb) < n_active, block_e, jnp.minimum(last_e, N_EXPERTS - 1))
    first = jnp.concatenate([jnp.ones((1,), jnp.int32),
                             (block_e[1:] != block_e[:-1]).astype(jnp.int32)])
    later = (experts[None, :] > experts[:, None]) & (counts[None, :] > 0)
    next_e = jnp.min(jnp.where(later, experts[None, :], N_EXPERTS), axis=-1)
    next_e = jnp.where(next_e == N_EXPERTS, experts, next_e)
    block_next = jnp.sum(jnp.where(block_e[:, None] == experts[None, :], next_e[None, :], 0),
                         axis=-1)
    tail_pieces = (nb * MOE_BLOCK - pad_end[-1]) // (MOE_BLOCK // 2)
    zero_start = jnp.concatenate([pad_start + counts, pad_end[-1:]]).astype(jnp.int32)
    zero_cnt = jnp.concatenate([padded - counts, tail_pieces[None]]).astype(jnp.int32)
    blocks = (block_e.astype(jnp.int32), block_next.astype(jnp.int32), first,
              n_active.reshape(1))
    return dest.astype(jnp.int32), blocks, zero_start, zero_cnt


def _pack_halves(v):
    half = v.shape[-1] // 2
    bits = lambda a: lax.bitcast_convert_type(a.astype(BF16).astype(F32), jnp.uint32)
    return (bits(v[:, :half]) >> 16) | (bits(v[:, half:]) & jnp.uint32(0xFFFF0000))


def _unpack_halves(w):
    lo = lax.bitcast_convert_type(w << 16, F32)
    hi = lax.bitcast_convert_type(w & jnp.uint32(0xFFFF0000), F32)
    return lo, hi


def _dispatch_kernel(zs_ref, zn_ref, dest_ref, h_ref, xs_hbm, buf, zbuf, sem, zsem, *, tm):
    i = pl.program_id(0)
    n = pl.num_programs(0)
    slot = lax.rem(i, 2)

    def zero_rows(wait):
        def zero_copy(off, rows):
            cp = pltpu.make_async_copy(zbuf.at[pl.ds(0, rows)], xs_hbm.at[pl.ds(off, rows)], zsem)
            if wait:
                cp.wait()
            else:
                cp.start()

        def single_rows(off, count):
            def one(j, carry):
                zero_copy(off + j, 1)
                return carry
            lax.fori_loop(0, count, one, 0)

        def body(e, carry):
            cnt = zn_ref[e]
            start = zs_ref[e]
            head = jnp.minimum((-start) & (SUBLANES - 1), cnt)
            single_rows(start, head)
            rest = cnt - head
            off = start + head
            rows = MOE_BLOCK // 2
            while rows >= SUBLANES:
                @pl.when((rest & rows) != 0)
                def _():
                    zero_copy(pl.multiple_of(off, SUBLANES), rows)
                off = off + (rest & rows)
                rows //= 2
            single_rows(off, rest & (SUBLANES - 1))
            return carry
        lax.fori_loop(0, N_EXPERTS, body, 0)

        def tail(c, carry):
            zero_copy(pl.multiple_of(zs_ref[N_EXPERTS] + c * (MOE_BLOCK // 2), MOE_BLOCK // 2),
                      MOE_BLOCK // 2)
            return carry
        lax.fori_loop(0, zn_ref[N_EXPERTS], tail, 0)

    @pl.when(i == 0)
    def _():
        zbuf[...] = jnp.zeros_like(zbuf)
        zero_rows(False)
        zero_rows(True)

    def wait_rows(s):
        for _ in range(TOP_K):
            pltpu.make_async_copy(buf.at[s], xs_hbm.at[pl.ds(0, tm)], sem.at[s]).wait()

    @pl.when(i >= 2)
    def _():
        wait_rows(slot)

    buf[slot] = _pack_halves(h_ref[...])

    def body(r, carry):
        for k in range(TOP_K):
            pltpu.make_async_copy(buf.at[slot, pl.ds(r, 1)],
                                  xs_hbm.at[pl.ds(dest_ref[0, 0, r * TOP_K + k], 1)],
                                  sem.at[slot]).start()
        return carry
    lax.fori_loop(0, tm, body, 0)

    @pl.when(i == n - 1)
    def _():
        wait_rows(slot)

    @pl.when((i == n - 1) & (n >= 2))
    def _():
        wait_rows(1 - slot)


def _dispatch(h, dest, zero_start, zero_cnt, nb):
    t, d = h.shape
    tm = min(DISPATCH_TM, t)
    grid_spec = pltpu.PrefetchScalarGridSpec(
        num_scalar_prefetch=2,
        grid=(t // tm,),
        in_specs=[pl.BlockSpec((1, 1, tm * TOP_K), lambda i, zs, zn: (i, 0, 0),
                               memory_space=pltpu.SMEM),
                  pl.BlockSpec((tm, d), lambda i, zs, zn: (i, 0))],
        out_specs=pl.BlockSpec(memory_space=pl.ANY),
        scratch_shapes=[pltpu.VMEM((2, tm, d // 2), jnp.uint32),
                        pltpu.VMEM((MOE_BLOCK // 2, d // 2), jnp.uint32),
                        pltpu.SemaphoreType.DMA((2,)),
                        pltpu.SemaphoreType.DMA(())],
    )
    return pl.pallas_call(
        functools.partial(_dispatch_kernel, tm=tm),
        grid_spec=grid_spec,
        out_shape=jax.ShapeDtypeStruct((nb * MOE_BLOCK, d // 2), jnp.uint32),
        compiler_params=_params(1),
        name="dispatch",
    )(zero_start, zero_cnt, dest.reshape(t // tm, 1, tm * TOP_K), h)


def _expert_kernel(be_ref, nx_ref, first_ref, na_ref, x_ref, wg_hbm, wu_hbm, wd_hbm, o_ref,
                   sg, su, sd, wg_ref, wu_ref, wd_ref, sem):
    i = pl.program_id(0)
    staged = ((wg_hbm, sg, wg_ref), (wu_hbm, su, wu_ref), (wd_hbm, sd, wd_ref))

    def fetch(e):
        for s, (src, stage, _) in enumerate(staged):
            pltpu.make_async_copy(src.at[e], stage, sem.at[s]).start()

    @pl.when(i < na_ref[0])
    def _():
        @pl.when(i == 0)
        def _():
            fetch(be_ref[0])

        @pl.when(first_ref[i] == 1)
        def _():
            for s, (src, stage, work) in enumerate(staged):
                pltpu.make_async_copy(src.at[0], stage, sem.at[s]).wait()
                work[...] = stage[...].astype(BF16)

            @pl.when(nx_ref[i] != be_ref[i])
            def _():
                fetch(nx_ref[i])

        lo, hi = _unpack_halves(x_ref[...])
        lo, hi = lo.astype(BF16), hi.astype(BF16)
        half = lo.shape[-1]

        def in_proj(w_ref):
            return (jnp.dot(lo, w_ref[:half, :], preferred_element_type=F32)
                    + jnp.dot(hi, w_ref[half:, :], preferred_element_type=F32))

        g = in_proj(wg_ref)
        u = in_proj(wu_ref)
        act = (g * _sigmoid(g) * u).astype(BF16)
        o_ref[...] = _pack_halves(jnp.dot(act, wd_ref[...], preferred_element_type=F32))

    @pl.when(i >= na_ref[0])
    def _():
        o_ref[...] = jnp.zeros_like(o_ref)


def _routed_experts(xs, blocks, wg, wu, wd):
    p, dh = xs.shape
    nb = p // MOE_BLOCK
    d, e_dim = wg.shape[-2:]
    anywhere = pl.BlockSpec(memory_space=pl.ANY)
    grid_spec = pltpu.PrefetchScalarGridSpec(
        num_scalar_prefetch=4,
        grid=(nb,),
        in_specs=[pl.BlockSpec((MOE_BLOCK, dh),
                               lambda i, be, nx, ff, na:
                               (jnp.minimum(i, jnp.maximum(na[0] - 1, 0)), 0)),
                  anywhere, anywhere, anywhere],
        out_specs=pl.BlockSpec((MOE_BLOCK, dh), lambda i, be, nx, ff, na: (i, 0)),
        scratch_shapes=[pltpu.VMEM((d, e_dim), F32), pltpu.VMEM((d, e_dim), F32),
                        pltpu.VMEM((e_dim, d), F32),
                        pltpu.VMEM((d, e_dim), BF16), pltpu.VMEM((d, e_dim), BF16),
                        pltpu.VMEM((e_dim, d), BF16),
                        pltpu.SemaphoreType.DMA((3,))],
    )
    return pl.pallas_call(
        _expert_kernel,
        grid_spec=grid_spec,
        out_shape=jax.ShapeDtypeStruct((p, dh), jnp.uint32),
        compiler_params=_params(1, EXPERT_VMEM_LIMIT),
        name="routed_experts",
    )(*blocks, xs, wg, wu, wd)


def _shared_kernel(h_ref, wg_ref, wu_ref, wd_ref, o_ref):
    x = h_ref[...].astype(BF16)
    g = jnp.dot(x, wg_ref[...], preferred_element_type=F32)
    u = jnp.dot(x, wu_ref[...], preferred_element_type=F32)
    act = (g * _sigmoid(g) * u).astype(BF16)
    o_ref[...] = jnp.dot(act, wd_ref[...], preferred_element_type=F32)


def _shared_expert(h, wg, wu, wd):
    t, d = h.shape
    e_dim = wg.shape[-1]
    tm = min(SHARED_TM, t)
    rowt = pl.BlockSpec((tm, d), lambda i: (i, 0))
    return pl.pallas_call(
        _shared_kernel,
        grid=(t // tm,),
        in_specs=[rowt,
                  pl.BlockSpec((d, e_dim), lambda i: (0, 0)),
                  pl.BlockSpec((d, e_dim), lambda i: (0, 0)),
                  pl.BlockSpec((e_dim, d), lambda i: (0, 0))],
        out_specs=rowt,
        out_shape=jax.ShapeDtypeStruct((t, d), F32),
        compiler_params=_params(1),
        name="shared_expert",
    )(h, wg, wu, wd)


def _final_kernel(dest_ref, destn_ref, ys_hbm, h_ref, s_ref, tw_ref, lw_ref, lb_ref, o_ref,
                  gbuf, sem, *, tm):
    i = pl.program_id(0)
    n = pl.num_programs(0)
    slot = lax.rem(i, 2)

    def row_copy(idx_ref, s, r, k):
        return pltpu.make_async_copy(ys_hbm.at[pl.ds(idx_ref[0, 0, r * TOP_K + k], 1)],
                                     gbuf.at[s, k, pl.ds(r, 1)], sem.at[s])

    def gather_wait(s):
        for k in range(TOP_K):
            pltpu.make_async_copy(ys_hbm.at[pl.ds(0, tm)], gbuf.at[s, k], sem.at[s]).wait()

    @pl.when(i == 0)
    def _():
        def body(r, carry):
            for k in range(TOP_K):
                row_copy(dest_ref, 0, r, k).start()
            return carry
        lax.fori_loop(0, tm, body, 0)

    gather_wait(slot)
    for r in range(tm):
        for k in range(TOP_K):
            row_copy(destn_ref, 1 - slot, r, k).start()

    tw = tw_ref[...]
    half = gbuf.shape[-1]
    y_lo = DN_ALPHA * h_ref[:, :half] + s_ref[:, :half]
    y_hi = DN_ALPHA * h_ref[:, half:] + s_ref[:, half:]
    for k in range(TOP_K):
        lo, hi = _unpack_halves(gbuf[slot, k])
        y_lo = y_lo + tw[:, k:k + 1] * lo
        y_hi = y_hi + tw[:, k:k + 1] * hi
    inv_d = 1.0 / (2 * half)
    mu = (jnp.sum(y_lo, axis=-1, keepdims=True) + jnp.sum(y_hi, axis=-1, keepdims=True)) * inv_d
    c_lo, c_hi = y_lo - mu, y_hi - mu
    var = (jnp.sum(c_lo * c_lo, axis=-1, keepdims=True)
           + jnp.sum(c_hi * c_hi, axis=-1, keepdims=True)) * inv_d
    r = lax.rsqrt(var + EPS)
    o_ref[:, :half] = c_lo * r * lw_ref[:, :half] + lb_ref[:, :half]
    o_ref[:, half:] = c_hi * r * lw_ref[:, half:] + lb_ref[:, half:]

    @pl.when(i == n - 1)
    def _():
        gather_wait(1 - slot)


def _final(ys, dest, h, shared, top_w, ln_w, ln_b):
    t, d = h.shape
    tm = min(FINAL_TM, t)
    n = t // tm
    idx = dest.reshape(n, 1, tm * TOP_K)
    idx_spec = lambda fn: pl.BlockSpec((1, 1, tm * TOP_K), fn, memory_space=pltpu.SMEM)
    rowt = pl.BlockSpec((tm, d), lambda i: (i, 0))
    vec = pl.BlockSpec((1, d), lambda i: (0, 0))
    return pl.pallas_call(
        functools.partial(_final_kernel, tm=tm),
        grid=(n,),
        in_specs=[idx_spec(lambda i: (i, 0, 0)),
                  idx_spec(lambda i: (jnp.minimum(i + 1, n - 1), 0, 0)),
                  pl.BlockSpec(memory_space=pl.ANY),
                  rowt, rowt,
                  pl.BlockSpec((tm, LANES), lambda i: (i, 0)),
                  vec, vec],
        out_specs=rowt,
        out_shape=jax.ShapeDtypeStruct((t, d), F32),
        scratch_shapes=[pltpu.VMEM((2, TOP_K, tm, d // 2), jnp.uint32),
                        pltpu.SemaphoreType.DMA((2,))],
        compiler_params=_params(1),
        name="combine_ln2",
    )(idx, idx, ys, h, shared, top_w, ln_w, ln_b)


def _rotary_tables(positions):
    pos = positions.reshape(-1).astype(F32)[:, None]
    ret_freqs = 1.0 / (RET_THETA ** jnp.linspace(0.0, 1.0, RET_QK_DIM // 2, dtype=F32))
    ang = pos * ret_freqs
    ret_rot = (jnp.cos(ang), jnp.sin(ang))
    rope_freqs = ROPE_THETA ** (-jnp.arange(0, DIFF_QK_DIM, 2, dtype=F32) / DIFF_QK_DIM)
    ang = pos * rope_freqs
    cos, sin = jnp.cos(ang), jnp.sin(ang)
    diff_rot = (jnp.concatenate([cos, cos], axis=-1), jnp.concatenate([-sin, sin], axis=-1))
    return ret_rot, diff_rot


def _layer(h, positions, w_in, w_ret_proj, w_diff_proj, w_out, lq1, lk1, lq2, lk2, diff_norm_w,
           ln1_w, ln1_b, w_router, router_bias, exp_gate, exp_up, exp_down,
           shared_gate, shared_up, shared_down, ln2_w, ln2_b, lambda_init):
    batch, seq, d = h.shape
    t = batch * seq
    x = h.reshape(t, d)
    xb = x.astype(BF16)
    ret_rot, diff_rot = _rotary_tables(positions)

    off = 0
    rq = _proj(xb, w_in, off, RET_QW, "ret_rot", 1.0, ret_rot); off += RET_QW
    rk = _proj(xb, w_in, off, RET_QW, "ret_rot", RET_QK_DIM ** -0.5, ret_rot); off += RET_QW
    rv = _proj(xb, w_in, off, RET_VW, "plain"); off += RET_VW
    rg = _proj(xb, w_in, off, RET_VW, "silu"); off += RET_VW
    dq = _proj(xb, w_in, off, DIFF_QW, "diff_rot", LOG2_E * DIFF_QK_DIM ** -0.5, diff_rot)
    off += DIFF_QW
    dk = _proj(xb, w_in, off, DIFF_QW, "diff_rot", 1.0, diff_rot); off += DIFF_QW
    dv = _proj(xb, w_in, off, DIFF_VW, "plain"); off += DIFF_VW
    gate_ret = _proj(xb, w_in, off, d, "sigmoid"); off += d
    gate_diff = _proj(xb, w_in, off, d, "sigmoid")

    ret = _retention(rq, rk, rv, rg, batch, seq)
    row = lambda v: v.reshape(1, -1).astype(F32)
    da = _diff_attention(dq, dk, dv, row(lq1), row(lk1), row(lq2), row(lk2), row(diff_norm_w),
                         batch, seq, lambda_init)
    merged = _merge(ret, da, w_ret_proj.astype(BF16), w_diff_proj.astype(BF16),
                    gate_ret, gate_diff)
    y1 = _outproj(merged, w_out, x)
    h1, top_e, top_w, rank, counts = _ln_route(y1, row(ln1_w), row(ln1_b), w_router,
                                               row(router_bias))

    dest, blocks, zero_start, zero_cnt = _expert_plan(
        top_e[:, :TOP_K], rank[:, :TOP_K], counts, t)
    xs = _dispatch(h1, dest, zero_start, zero_cnt, blocks[0].shape[0])
    ys = _routed_experts(xs, blocks, exp_gate, exp_up, exp_down)
    shared = _shared_expert(h1, shared_gate.astype(BF16), shared_up.astype(BF16),
                            shared_down.astype(BF16))
    out = _final(ys, dest, h1, shared, top_w, row(ln2_w), row(ln2_b))
    return out.reshape(batch, seq, d)


def kernel(x, positions, w_in, w_ret_proj, w_diff_proj, w_out, lambda_q1, lambda_k1, lambda_q2,
           lambda_k2, diff_norm_w, ln1_w, ln1_b, w_router, router_bias, exp_gate, exp_up,
           exp_down, shared_gate, shared_up, shared_down, ln2_w, ln2_b):
    h = x
    for l in range(w_in.shape[0]):
        lambda_init = 0.8 - 0.6 * math.exp(-0.3 * l)
        h = _layer(h, positions, w_in[l], w_ret_proj[l], w_diff_proj[l], w_out[l],
                   lambda_q1[l], lambda_k1[l], lambda_q2[l], lambda_k2[l], diff_norm_w[l],
                   ln1_w[l], ln1_b[l], w_router[l], router_bias[l], exp_gate[l], exp_up[l],
                   exp_down[l], shared_gate[l], shared_up[l], shared_down[l],
                   ln2_w[l], ln2_b[l], lambda_init)
    return h
```

```python
import functools
import math

import jax
import jax.numpy as jnp
from jax import lax
from jax.experimental import pallas as pl
from jax.experimental.pallas import tpu as pltpu

F32 = jnp.float32
BF16 = jnp.bfloat16

D_MODEL = 4096
CHUNK = 64
RET_HEADS = 8
RET_QK_DIM = 256
RET_V_DIM = 512
RET_THETA = 10000.0
DIFF_HEADS = 16
DIFF_QK_DIM = 128
DIFF_V_DIM = 256
ROPE_THETA = 10000.0
RET_QW = RET_HEADS * RET_QK_DIM
RET_VW = RET_HEADS * RET_V_DIM
DIFF_QW = DIFF_HEADS * 2 * DIFF_QK_DIM
DIFF_VW = DIFF_HEADS * DIFF_V_DIM
N_EXPERTS = 64
TOP_K = 8
N_GROUPS = 8
TOPK_GROUPS = 4
EXPERT_DIM = 512
ROUTED_SCALE = 2.5
DEPTH = 1
DN_ALPHA = (2.0 * DEPTH) ** 0.25
EPS = 1e-5
LOG2_E = math.log2(math.e)

LANES = 128
SUBLANES = 8
MIB = 1024 * 1024
VMEM_LIMIT = 56 * MIB
EXPERT_VMEM_LIMIT = 60 * MIB

PROJ_TM, PROJ_TN = 1024, 512
MERGE_TM, MERGE_TN = 512, 512
RET_BLOCK = 256
RET_HEADS_PER_STEP = 2
ATT_BLOCK = 512
LN_TM = 256
MOE_BLOCK = 256
DISPATCH_TM = 128
FINAL_TM = 128


def _params(n_axes, vmem_limit=VMEM_LIMIT):
    return pltpu.CompilerParams(dimension_semantics=("arbitrary",) * n_axes,
                                vmem_limit_bytes=vmem_limit)


def _sigmoid(v):
    return 1.0 / (1.0 + jnp.exp(-v))


def _proj_kernel(*refs, mode, scale, tn):
    if mode in ("ret_rot", "diff_rot"):
        x_ref, w_ref, c_ref, s_ref, o_ref = refs
    else:
        x_ref, w_ref, o_ref = refs
    acc = jnp.dot(x_ref[...], w_ref[...].astype(BF16), preferred_element_type=F32)
    if mode == "ret_rot":
        cos, sin = c_ref[...], s_ref[...]
        for h in range(tn // RET_QK_DIM):
            lo = h * RET_QK_DIM
            x1 = acc[:, lo:lo + LANES]
            x2 = acc[:, lo + LANES:lo + 2 * LANES]
            o_ref[:, lo:lo + LANES] = ((x1 * cos - x2 * sin) * scale).astype(o_ref.dtype)
            o_ref[:, lo + LANES:lo + 2 * LANES] = ((x2 * cos + x1 * sin) * scale).astype(o_ref.dtype)
    elif mode == "diff_rot":
        c, s = c_ref[...], s_ref[...]
        for g in range(tn // LANES):
            xg = acc[:, g * LANES:(g + 1) * LANES]
            rot = pltpu.roll(xg, LANES // 2, axis=1)
            o_ref[:, g * LANES:(g + 1) * LANES] = ((xg * c + rot * s) * scale).astype(o_ref.dtype)
    elif mode == "silu":
        o_ref[...] = (acc * _sigmoid(acc)).astype(o_ref.dtype)
    elif mode == "sigmoid":
        o_ref[...] = _sigmoid(acc).astype(o_ref.dtype)
    else:
        o_ref[...] = acc.astype(o_ref.dtype)


def _proj(xb, w, col_off, width, mode, scale=1.0, rot=None):
    t, d = xb.shape
    tm, tn = min(PROJ_TM, t), PROJ_TN
    joff = col_off // tn
    in_specs = [pl.BlockSpec((tm, d), lambda i, j: (i, 0)),
                pl.BlockSpec((d, tn), lambda i, j: (0, j + joff))]
    args = [xb, w]
    if rot is not None:
        in_specs += [pl.BlockSpec((tm, LANES), lambda i, j: (i, 0))] * 2
        args += list(rot)
    return pl.pallas_call(
        functools.partial(_proj_kernel, mode=mode, scale=scale, tn=tn),
        grid=(t // tm, width // tn),
        in_specs=in_specs,
        out_specs=pl.BlockSpec((tm, tn), lambda i, j: (i, j)),
        out_shape=jax.ShapeDtypeStruct((t, width), BF16),
        compiler_params=_params(2),
        name="proj_" + mode,
    )(*args)


def _ret_kernel(q_ref, k_ref, v_ref, g_ref, dm_ref, qd_ref, kd_ref, cd_ref, o_ref, r_ref):
    @pl.when(pl.program_id(2) == 0)
    def _():
        r_ref[...] = jnp.zeros_like(r_ref)

    for hh in range(RET_HEADS_PER_STEP):
        qk = slice(hh * RET_QK_DIM, (hh + 1) * RET_QK_DIM)
        vv = slice(hh * RET_V_DIM, (hh + 1) * RET_V_DIM)
        q, k, v = q_ref[:, qk], k_ref[:, qk], v_ref[:, vv]
        s = lax.dot_general(q, k, (((1,), (1,)), ((), ())), preferred_element_type=F32)
        s = (s * dm_ref[hh]).astype(BF16)
        inner = jnp.dot(s, v, preferred_element_type=F32)
        qs = (q.astype(F32) * qd_ref[hh]).astype(BF16)
        r = r_ref[hh]
        cross = jnp.dot(qs, r.astype(BF16), preferred_element_type=F32)
        ks = (k.astype(F32) * kd_ref[hh]).astype(BF16)
        r_ref[hh] = r * cd_ref[hh] + lax.dot_general(
            ks, v, (((0,), (0,)), ((), ())), preferred_element_type=F32)
        y = inner + cross
        mu = jnp.mean(y, axis=-1, keepdims=True)
        yc = y - mu
        var = jnp.mean(yc * yc, axis=-1, keepdims=True)
        o_ref[:, vv] = (yc * lax.rsqrt(var + EPS) * g_ref[:, vv].astype(F32)).astype(o_ref.dtype)


def _retention_tables(blk):
    log_g = jnp.log1p(-jnp.exp2(-5.0 - jnp.arange(RET_HEADS, dtype=F32)))
    idx = jnp.arange(blk, dtype=F32)
    chunk = jnp.arange(blk) // CHUNK
    visible = chunk[None, :] <= chunk[:, None]
    dm = jnp.where(visible[None],
                   jnp.exp(log_g[:, None, None] * jnp.abs(idx[:, None] - idx[None, :])), 0.0)
    qd = jnp.exp(log_g[:, None] * (idx + 1.0))
    kd = jnp.exp(log_g[:, None] * (blk - 1.0 - idx))
    cd = jnp.exp(log_g * blk)
    qd = jnp.broadcast_to(qd[:, :, None], (RET_HEADS, blk, RET_QK_DIM))
    kd = jnp.broadcast_to(kd[:, :, None], (RET_HEADS, blk, RET_QK_DIM))
    cd = jnp.broadcast_to(cd[:, None, None], (RET_HEADS, 1, RET_V_DIM))
    return dm, qd, kd, cd


def _retention(rq, rk, rv, rg, batch, seq):
    t = batch * seq
    blk = min(RET_BLOCK, seq)
    nl = seq // blk
    dm, qd, kd, cd = _retention_tables(blk)
    hs = RET_HEADS_PER_STEP
    row = lambda b, h, l: (b * nl + l, h)
    head = lambda b, h, l: (h, 0, 0)
    return pl.pallas_call(
        _ret_kernel,
        grid=(batch, RET_HEADS // hs, nl),
        in_specs=[pl.BlockSpec((blk, hs * RET_QK_DIM), row),
                  pl.BlockSpec((blk, hs * RET_QK_DIM), row),
                  pl.BlockSpec((blk, hs * RET_V_DIM), row),
                  pl.BlockSpec((blk, hs * RET_V_DIM), row),
                  pl.BlockSpec((hs, blk, blk), head),
                  pl.BlockSpec((hs, blk, RET_QK_DIM), head),
                  pl.BlockSpec((hs, blk, RET_QK_DIM), head),
                  pl.BlockSpec((hs, 1, RET_V_DIM), head)],
        out_specs=pl.BlockSpec((blk, hs * RET_V_DIM), row),
        out_shape=jax.ShapeDtypeStruct((t, RET_VW), BF16),
        scratch_shapes=[pltpu.VMEM((hs, RET_QK_DIM, RET_V_DIM), F32)],
        compiler_params=_params(3),
        name="retention",
    )(rq, rk, rv, rg, dm, qd, kd, cd)


def _attn_kernel(lq1_ref, lk1_ref, lq2_ref, lk2_ref, nw_ref, q_ref, k_ref, v_ref, o_ref,
                 m_ref, l_ref, acc_ref, *, blk, lambda_init):
    qi = pl.program_id(2)
    m_ref[...] = jnp.full_like(m_ref, -jnp.inf)
    l_ref[...] = jnp.zeros_like(l_ref)
    acc_ref[...] = jnp.zeros_like(acc_ref)
    q = q_ref[...]

    def block(first_key, width, masked):
        off = pl.multiple_of(first_key, width)
        kb = k_ref[pl.ds(off, width), :]
        vb = v_ref[pl.ds(off, width), :]
        if masked:
            rchunk = lax.broadcasted_iota(jnp.int32, (blk, width), 0) // CHUNK
            cchunk = lax.broadcasted_iota(jnp.int32, (blk, width), 1) // CHUNK
            visible = cchunk <= rchunk
        for c in range(2):
            qc = q[:, c * DIFF_QK_DIM:(c + 1) * DIFF_QK_DIM]
            kc = kb[:, c * DIFF_QK_DIM:(c + 1) * DIFF_QK_DIM]
            s = lax.dot_general(qc, kc, (((1,), (1,)), ((), ())), preferred_element_type=F32)
            if masked:
                s = jnp.where(visible, s, -jnp.inf)
            m_prev = m_ref[c]
            m_new = jnp.maximum(m_prev, jnp.max(s, axis=-1, keepdims=True))
            p = jnp.exp2(s - jnp.tile(m_new, (1, width // LANES)))
            alpha = jnp.exp2(m_prev - m_new)
            l_ref[c] = alpha * l_ref[c] + jnp.sum(p, axis=-1, keepdims=True)
            acc_ref[c] = (jnp.tile(alpha, (1, DIFF_V_DIM // LANES)) * acc_ref[c]
                          + jnp.dot(p.astype(BF16), vb, preferred_element_type=F32))
            m_ref[c] = m_new

    def body(j, carry):
        block(j * (2 * blk), 2 * blk, False)
        return carry

    lax.fori_loop(0, qi // 2, body, 0)

    @pl.when(lax.rem(qi, 2) == 1)
    def _():
        block((qi - 1) * blk, blk, False)

    block(qi * blk, blk, True)

    lam = (jnp.exp(jnp.sum(lq1_ref[...] * lk1_ref[...], axis=-1, keepdims=True))
           - jnp.exp(jnp.sum(lq2_ref[...] * lk2_ref[...], axis=-1, keepdims=True))
           + lambda_init)
    rep = DIFF_V_DIM // LANES
    o = (acc_ref[0] / jnp.tile(l_ref[0], (1, rep))
         - lam * (acc_ref[1] / jnp.tile(l_ref[1], (1, rep))))
    ms = jnp.mean(o * o, axis=-1, keepdims=True)
    o_ref[...] = (o * lax.rsqrt(ms + EPS) * nw_ref[...] * (1.0 - lambda_init)).astype(o_ref.dtype)


def _diff_attention(dq, dk, dv, lq1, lk1, lq2, lk2, norm_w, batch, seq, lambda_init):
    t = batch * seq
    blk = min(ATT_BLOCK, seq)
    nq = seq // blk
    hw = 2 * DIFF_QK_DIM
    vec = lambda b, h, i: (0, 0)
    return pl.pallas_call(
        functools.partial(_attn_kernel, blk=blk, lambda_init=lambda_init),
        grid=(batch, DIFF_HEADS, nq),
        in_specs=[pl.BlockSpec((1, DIFF_QK_DIM), vec)] * 4
        + [pl.BlockSpec((1, DIFF_V_DIM), vec),
           pl.BlockSpec((blk, hw), lambda b, h, i: (b * nq + i, h)),
           pl.BlockSpec((seq, hw), lambda b, h, i: (b, h)),
           pl.BlockSpec((seq, DIFF_V_DIM), lambda b, h, i: (b, h))],
        out_specs=pl.BlockSpec((blk, DIFF_V_DIM), lambda b, h, i: (b * nq + i, h)),
        out_shape=jax.ShapeDtypeStruct((t, DIFF_VW), BF16),
        scratch_shapes=[pltpu.VMEM((2, blk, LANES), F32),
                        pltpu.VMEM((2, blk, LANES), F32),
                        pltpu.VMEM((2, blk, DIFF_V_DIM), F32)],
        compiler_params=_params(3),
        name="diff_attention",
    )(lq1, lk1, lq2, lk2, norm_w, dq, dk, dv)


def _merge_kernel(r_ref, d_ref, wr_ref, wd_ref, gr_ref, gd_ref, o_ref):
    a = jnp.dot(r_ref[...], wr_ref[...], preferred_element_type=F32)
    b = jnp.dot(d_ref[...], wd_ref[...], preferred_element_type=F32)
    o_ref[...] = (gr_ref[...].astype(F32) * a + gd_ref[...].astype(F32) * b).astype(o_ref.dtype)


def _merge(ret, da, wr, wd, gr, gd):
    t, kdim = ret.shape
    n = wr.shape[1]
    tm, tn = min(MERGE_TM, t), MERGE_TN
    lhs = pl.BlockSpec((tm, kdim), lambda i, j: (i, 0))
    rhs = pl.BlockSpec((kdim, tn), lambda i, j: (0, j))
    tile = pl.BlockSpec((tm, tn), lambda i, j: (i, j))
    return pl.pallas_call(
        _merge_kernel,
        grid=(t // tm, n // tn),
        in_specs=[lhs, lhs, rhs, rhs, tile, tile],
        out_specs=tile,
        out_shape=jax.ShapeDtypeStruct((t, n), BF16),
        compiler_params=_params(2),
        name="merge",
    )(ret, da, wr, wd, gr, gd)


def _outproj_kernel(m_ref, w_ref, x_ref, o_ref):
    o_ref[...] = DN_ALPHA * x_ref[...] + jnp.dot(m_ref[...], w_ref[...].astype(BF16),
                                                  preferred_element_type=F32)


def _outproj(merged, w, x):
    t, kdim = merged.shape
    n = w.shape[1]
    tm, tn = min(PROJ_TM, t), PROJ_TN
    tile = pl.BlockSpec((tm, tn), lambda i, j: (i, j))
    return pl.pallas_call(
        _outproj_kernel,
        grid=(t // tm, n // tn),
        in_specs=[pl.BlockSpec((tm, kdim), lambda i, j: (i, 0)),
                  pl.BlockSpec((kdim, tn), lambda i, j: (0, j)),
                  tile],
        out_specs=tile,
        out_shape=jax.ShapeDtypeStruct((t, n), F32),
        compiler_params=_params(2),
        name="outproj",
    )(merged, w, x)


def _layernorm(y, w, b):
    mu = jnp.mean(y, axis=-1, keepdims=True)
    yc = y - mu
    var = jnp.mean(yc * yc, axis=-1, keepdims=True)
    return yc * lax.rsqrt(var + EPS) * w + b


def _ln_route_kernel(y_ref, lw_ref, lb_ref, wr_ref, rb_ref, h_ref, te_ref, tw_ref, tr_ref, tc_ref,
                     cnt_ref):
    h = _layernorm(y_ref[...], lw_ref[...], lb_ref[...])
    h_ref[...] = h
    hi = h.astype(BF16)
    lo = (h - hi.astype(F32)).astype(BF16)
    w = wr_ref[...]
    whi = w.astype(BF16)
    wlo = (w - whi.astype(F32)).astype(BF16)
    logits = (jnp.dot(hi, whi, preferred_element_type=F32)
              + (jnp.dot(hi, wlo, preferred_element_type=F32)
                 + jnp.dot(lo, whi, preferred_element_type=F32)))
    scores = _sigmoid(logits)
    biased = scores + rb_ref[...]
    tm = scores.shape[0]
    neg = -jnp.inf
    lane_i = lax.broadcasted_iota(jnp.int32, (tm, N_EXPERTS), 1)
    lane = lane_i.astype(F32)
    per_group = N_EXPERTS // N_GROUPS
    grp = lane_i // per_group

    gscores = []
    gs_lane = jnp.zeros((tm, N_EXPERTS), F32)
    for g in range(N_GROUPS):
        mk = grp == g
        vals = jnp.where(mk, biased, neg)
        m1 = jnp.max(vals, axis=-1, keepdims=True)
        i1 = jnp.min(jnp.where(vals == m1, lane, float(N_EXPERTS)), axis=-1, keepdims=True)
        m2 = jnp.max(jnp.where(lane == i1, neg, vals), axis=-1, keepdims=True)
        gscores.append(m1 + m2)
        gs_lane = jnp.where(mk, m1 + m2, gs_lane)
    beaten = jnp.zeros((tm, N_EXPERTS), F32)
    for g in range(N_GROUPS):
        wins = jnp.where(gscores[g] > gs_lane, 1.0,
                         jnp.where(gscores[g] == gs_lane,
                                   jnp.where(grp > g, 1.0, 0.0), 0.0))
        beaten = beaten + wins
    cur = jnp.where(beaten < float(TOPK_GROUPS), biased, neg)

    out_lane = lax.broadcasted_iota(jnp.int32, (tm, LANES), 1)
    sel_e = jnp.zeros((tm, LANES), F32)
    sel_w = jnp.zeros((tm, LANES), F32)
    wsum = jnp.zeros((tm, 1), F32)
    hits = []
    for k in range(TOP_K):
        m = jnp.max(cur, axis=-1, keepdims=True)
        idx = jnp.min(jnp.where(cur == m, lane, float(N_EXPERTS)), axis=-1, keepdims=True)
        hit = lane == idx
        wk = jnp.sum(jnp.where(hit, scores, 0.0), axis=-1, keepdims=True)
        cur = jnp.where(hit, neg, cur)
        wsum = wsum + wk
        sel_e = jnp.where(out_lane == k, idx, sel_e)
        sel_w = jnp.where(out_lane == k, wk, sel_w)
        hits.append(hit)
    te_ref[...] = sel_e.astype(jnp.int32)
    tw_ref[...] = sel_w / wsum * ROUTED_SCALE

    @pl.when(pl.program_id(0) == 0)
    def _():
        cnt_ref[...] = jnp.zeros_like(cnt_ref)

    chosen = jnp.zeros((tm, N_EXPERTS), F32)
    for hit in hits:
        chosen = chosen + jnp.where(hit, 1.0, 0.0)
    earlier = (lax.broadcasted_iota(jnp.int32, (tm, tm), 0)
               > lax.broadcasted_iota(jnp.int32, (tm, tm), 1))
    prefix = jnp.dot(jnp.where(earlier, 1.0, 0.0).astype(BF16), chosen.astype(BF16),
                     preferred_element_type=F32) + cnt_ref[...]
    sel_r = jnp.zeros((tm, LANES), F32)
    for k, hit in enumerate(hits):
        rk = jnp.sum(jnp.where(hit, prefix, 0.0), axis=-1, keepdims=True)
        sel_r = jnp.where(out_lane == k, rk, sel_r)
    tr_ref[...] = sel_r.astype(jnp.int32)
    cnt_ref[...] = cnt_ref[...] + jnp.sum(chosen, axis=0, keepdims=True)
    tc_ref[...] = cnt_ref[...]


def _ln_route(y, ln_w, ln_b, w_router, router_bias):
    t, d = y.shape
    tm = min(LN_TM, t)
    rowt = pl.BlockSpec((tm, d), lambda i: (i, 0))
    vec = pl.BlockSpec((1, d), lambda i: (0, 0))
    narrow = pl.BlockSpec((tm, LANES), lambda i: (i, 0))
    return pl.pallas_call(
        _ln_route_kernel,
        grid=(t // tm,),
        in_specs=[rowt, vec, vec,
                  pl.BlockSpec((d, N_EXPERTS), lambda i: (0, 0)),
                  pl.BlockSpec((1, N_EXPERTS), lambda i: (0, 0))],
        out_specs=[rowt, narrow, narrow, narrow,
                   pl.BlockSpec((1, N_EXPERTS), lambda i: (0, 0))],
        out_shape=[jax.ShapeDtypeStruct((t, d), F32),
                   jax.ShapeDtypeStruct((t, LANES), jnp.int32),
                   jax.ShapeDtypeStruct((t, LANES), F32),
                   jax.ShapeDtypeStruct((t, LANES), jnp.int32),
                   jax.ShapeDtypeStruct((1, N_EXPERTS), F32)],
        scratch_shapes=[pltpu.VMEM((1, N_EXPERTS), F32)],
        compiler_params=_params(1),
        name="ln_route",
    )(y, ln_w, ln_b, w_router, router_bias)


def _expert_plan(top_e, rank, counts, t):
    nb = t * TOP_K // MOE_BLOCK + N_EXPERTS
    counts = counts.reshape(N_EXPERTS).astype(jnp.int32)
    padded = (counts + MOE_BLOCK - 1) // MOE_BLOCK * MOE_BLOCK
    pad_end = jnp.cumsum(padded)
    pad_start = pad_end - padded
    experts = jnp.arange(N_EXPERTS, dtype=jnp.int32)
    dest = jnp.sum(jnp.where(top_e[..., None] == experts, pad_start, 0), axis=-1) + rank
    n_active = (pad_end[-1] // MOE_BLOCK).astype(jnp.int32)
    blk_start = jnp.arange(nb, dtype=jnp.int32) * MOE_BLOCK
    block_e = jnp.sum((pad_end[None, :] <= blk_start[:, None]).astype(jnp.int32), axis=-1)
    block_e = jnp.minimum(block_e, N_EXPERTS - 1)
    last_e = jnp.sum((pad_end <= (n_active - 1) * MOE_BLOCK).astype(jnp.int32))
    block_e = jnp.where(jnp.arange(nb) < n_active, block_e, jnp.minimum(last_e, N_EXPERTS - 1))
    first = jnp.concatenate([jnp.ones((1,), jnp.int32),
                             (block_e[1:] != block_e[:-1]).astype(jnp.int32)])
    later = (experts[None, :] > experts[:, None]) & (counts[None, :] > 0)
    next_e = jnp.min(jnp.where(later, experts[None, :], N_EXPERTS), axis=-1)
    next_e = jnp.where(next_e == N_EXPERTS, experts, next_e)
    block_next = jnp.sum(jnp.where(block_e[:, None] == experts[None, :], next_e[None, :], 0),
                         axis=-1)
    tail_pieces = (nb * MOE_BLOCK - pad_end[-1]) // (MOE_BLOCK // 2)
    zero_start = jnp.concatenate([pad_start + counts, pad_end[-1:]]).astype(jnp.int32)
    zero_cnt = jnp.concatenate([padded - counts, tail_pieces[None]]).astype(jnp.int32)
    blocks = (block_e.astype(jnp.int32), block_next.astype(jnp.int32), first,
              n_active.reshape(1))
    return dest.astype(jnp.int32), blocks, zero_start, zero_cnt


def _pack_halves(v):
    half = v.shape[-1] // 2
    bits = lambda a: lax.bitcast_convert_type(a.astype(BF16).astype(F32), jnp.uint32)
    return (bits(v[:, :half]) >> 16) | (bits(v[:, half:]) & jnp.uint32(0xFFFF0000))


def _unpack_halves(w):
    lo = lax.bitcast_convert_type(w << 16, F32)
    hi = lax.bitcast_convert_type(w & jnp.uint32(0xFFFF0000), F32)
    return lo, hi


def _dispatch_kernel(zs_ref, zn_ref, dest_ref, h_ref, wg_ref, wu_ref, wd_ref, xs_hbm, s_ref,
                     buf, zbuf, sem, zsem, *, tm):
    i = pl.program_id(0)
    n = pl.num_programs(0)
    slot = lax.rem(i, 2)

    def zero_rows(wait):
        def zero_copy(off, rows):
            cp = pltpu.make_async_copy(zbuf.at[pl.ds(0, rows)], xs_hbm.at[pl.ds(off, rows)], zsem)
            if wait:
                cp.wait()
            else:
                cp.start()

        def single_rows(off, count):
            def one(j, carry):
                zero_copy(off + j, 1)
                return carry
            lax.fori_loop(0, count, one, 0)

        def body(e, carry):
            cnt = zn_ref[e]
            start = zs_ref[e]
            head = jnp.minimum((-start) & (SUBLANES - 1), cnt)
            single_rows(start, head)
            rest = cnt - head
            off = start + head
            rows = MOE_BLOCK // 2
            while rows >= SUBLANES:
                @pl.when((rest & rows) != 0)
                def _():
                    zero_copy(pl.multiple_of(off, SUBLANES), rows)
                off = off + (rest & rows)
                rows //= 2
            single_rows(off, rest & (SUBLANES - 1))
            return carry
        lax.fori_loop(0, N_EXPERTS, body, 0)

        def tail(c, carry):
            zero_copy(pl.multiple_of(zs_ref[N_EXPERTS] + c * (MOE_BLOCK // 2), MOE_BLOCK // 2),
                      MOE_BLOCK // 2)
            return carry
        lax.fori_loop(0, zn_ref[N_EXPERTS], tail, 0)

    @pl.when(i == 0)
    def _():
        zbuf[...] = jnp.zeros_like(zbuf)
        zero_rows(False)
        zero_rows(True)

    def wait_rows(s):
        for _ in range(TOP_K):
            pltpu.make_async_copy(buf.at[s], xs_hbm.at[pl.ds(0, tm)], sem.at[s]).wait()

    @pl.when(i >= 2)
    def _():
        wait_rows(slot)

    h = h_ref[...]
    buf[slot] = _pack_halves(h)
    for r in range(tm):
        for k in range(TOP_K):
            pltpu.make_async_copy(buf.at[slot, pl.ds(r, 1)],
                                  xs_hbm.at[pl.ds(dest_ref[0, 0, r * TOP_K + k], 1)],
                                  sem.at[slot]).start()

    x = h.astype(BF16)
    g = jnp.dot(x, wg_ref[...], preferred_element_type=F32)
    u = jnp.dot(x, wu_ref[...], preferred_element_type=F32)
    act = (g * _sigmoid(g) * u).astype(BF16)
    s_ref[...] = jnp.dot(act, wd_ref[...], preferred_element_type=F32)

    @pl.when(i == n - 1)
    def _():
        wait_rows(slot)

    @pl.when((i == n - 1) & (n >= 2))
    def _():
        wait_rows(1 - slot)


def _dispatch_shared(h, dest, zero_start, zero_cnt, nb, wg, wu, wd):
    t, d = h.shape
    e_dim = wg.shape[-1]
    tm = min(DISPATCH_TM, t)
    rowt = pl.BlockSpec((tm, d), lambda i, zs, zn: (i, 0))
    whole = lambda shape: pl.BlockSpec(shape, lambda i, zs, zn: (0, 0))
    grid_spec = pltpu.PrefetchScalarGridSpec(
        num_scalar_prefetch=2,
        grid=(t // tm,),
        in_specs=[pl.BlockSpec((1, 1, tm * TOP_K), lambda i, zs, zn: (i, 0, 0),
                               memory_space=pltpu.SMEM),
                  rowt, whole((d, e_dim)), whole((d, e_dim)), whole((e_dim, d))],
        out_specs=[pl.BlockSpec(memory_space=pl.ANY), rowt],
        scratch_shapes=[pltpu.VMEM((2, tm, d // 2), jnp.uint32),
                        pltpu.VMEM((MOE_BLOCK // 2, d // 2), jnp.uint32),
                        pltpu.SemaphoreType.DMA((2,)),
                        pltpu.SemaphoreType.DMA(())],
    )
    return pl.pallas_call(
        functools.partial(_dispatch_kernel, tm=tm),
        grid_spec=grid_spec,
        out_shape=[jax.ShapeDtypeStruct((nb * MOE_BLOCK, d // 2), jnp.uint32),
                   jax.ShapeDtypeStruct((t, d), F32)],
        compiler_params=_params(1),
        name="dispatch_shared",
    )(zero_start, zero_cnt, dest.reshape(t // tm, 1, tm * TOP_K), h, wg, wu, wd)


def _expert_kernel(be_ref, nx_ref, first_ref, na_ref, x_ref, wg_hbm, wu_hbm, wd_hbm, o_ref,
                   sg, su, sd, wg_ref, wu_ref, wd_ref, sem):
    i = pl.program_id(0)
    staged = ((wg_hbm, sg, wg_ref), (wu_hbm, su, wu_ref), (wd_hbm, sd, wd_ref))

    def fetch(e):
        for s, (src, stage, _) in enumerate(staged):
            pltpu.make_async_copy(src.at[e], stage, sem.at[s]).start()

    @pl.when(i < na_ref[0])
    def _():
        @pl.when(i == 0)
        def _():
            fetch(be_ref[0])

        @pl.when(first_ref[i] == 1)
        def _():
            for s, (src, stage, work) in enumerate(staged):
                pltpu.make_async_copy(src.at[0], stage, sem.at[s]).wait()
                work[...] = stage[...].astype(BF16)

            @pl.when(nx_ref[i] != be_ref[i])
            def _():
                fetch(nx_ref[i])

        lo, hi = _unpack_halves(x_ref[...])
        lo, hi = lo.astype(BF16), hi.astype(BF16)
        half = lo.shape[-1]

        def in_proj(w_ref):
            return (jnp.dot(lo, w_ref[:half, :], preferred_element_type=F32)
                    + jnp.dot(hi, w_ref[half:, :], preferred_element_type=F32))

        g = in_proj(wg_ref)
        u = in_proj(wu_ref)
        act = (g * _sigmoid(g) * u).astype(BF16)
        o_ref[...] = _pack_halves(jnp.dot(act, wd_ref[...], preferred_element_type=F32))

    @pl.when(i >= na_ref[0])
    def _():
        o_ref[...] = jnp.zeros_like(o_ref)


def _routed_experts(xs, blocks, wg, wu, wd):
    p, dh = xs.shape
    nb = p // MOE_BLOCK
    d, e_dim = wg.shape[-2:]
    anywhere = pl.BlockSpec(memory_space=pl.ANY)
    grid_spec = pltpu.PrefetchScalarGridSpec(
        num_scalar_prefetch=4,
        grid=(nb,),
        in_specs=[pl.BlockSpec((MOE_BLOCK, dh),
                               lambda i, be, nx, ff, na:
                               (jnp.minimum(i, jnp.maximum(na[0] - 1, 0)), 0)),
                  anywhere, anywhere, anywhere],
        out_specs=pl.BlockSpec((MOE_BLOCK, dh), lambda i, be, nx, ff, na: (i, 0)),
        scratch_shapes=[pltpu.VMEM((d, e_dim), F32), pltpu.VMEM((d, e_dim), F32),
                        pltpu.VMEM((e_dim, d), F32),
                        pltpu.VMEM((d, e_dim), BF16), pltpu.VMEM((d, e_dim), BF16),
                        pltpu.VMEM((e_dim, d), BF16),
                        pltpu.SemaphoreType.DMA((3,))],
    )
    return pl.pallas_call(
        _expert_kernel,
        grid_spec=grid_spec,
        out_shape=jax.ShapeDtypeStruct((p, dh), jnp.uint32),
        compiler_params=_params(1, EXPERT_VMEM_LIMIT),
        name="routed_experts",
    )(*blocks, xs, wg, wu, wd)


def _final_kernel(dest_ref, destn_ref, ys_hbm, h_ref, s_ref, tw_ref, lw_ref, lb_ref, o_ref,
                  gbuf, sem, *, tm):
    i = pl.program_id(0)
    n = pl.num_programs(0)
    slot = lax.rem(i, 2)

    def row_copy(idx_ref, s, r, k):
        return pltpu.make_async_copy(ys_hbm.at[pl.ds(idx_ref[0, 0, r * TOP_K + k], 1)],
                                     gbuf.at[s, k, pl.ds(r, 1)], sem.at[s])

    def gather_wait(s):
        for k in range(TOP_K):
            pltpu.make_async_copy(ys_hbm.at[pl.ds(0, tm)], gbuf.at[s, k], sem.at[s]).wait()

    @pl.when(i == 0)
    def _():
        def body(r, carry):
            for k in range(TOP_K):
                row_copy(dest_ref, 0, r, k).start()
            return carry
        lax.fori_loop(0, tm, body, 0)

    gather_wait(slot)
    for r in range(tm):
        for k in range(TOP_K):
            row_copy(destn_ref, 1 - slot, r, k).start()

    tw = tw_ref[...]
    half = gbuf.shape[-1]
    y_lo = DN_ALPHA * h_ref[:, :half] + s_ref[:, :half]
    y_hi = DN_ALPHA * h_ref[:, half:] + s_ref[:, half:]
    for k in range(TOP_K):
        lo, hi = _unpack_halves(gbuf[slot, k])
        y_lo = y_lo + tw[:, k:k + 1] * lo
        y_hi = y_hi + tw[:, k:k + 1] * hi
    inv_d = 1.0 / (2 * half)
    mu = (jnp.sum(y_lo, axis=-1, keepdims=True) + jnp.sum(y_hi, axis=-1, keepdims=True)) * inv_d
    c_lo, c_hi = y_lo - mu, y_hi - mu
    var = (jnp.sum(c_lo * c_lo, axis=-1, keepdims=True)
           + jnp.sum(c_hi * c_hi, axis=-1, keepdims=True)) * inv_d
    r = lax.rsqrt(var + EPS)
    o_ref[:, :half] = c_lo * r * lw_ref[:, :half] + lb_ref[:, :half]
    o_ref[:, half:] = c_hi * r * lw_ref[:, half:] + lb_ref[:, half:]

    @pl.when(i == n - 1)
    def _():
        gather_wait(1 - slot)


def _final(ys, dest, h, shared, top_w, ln_w, ln_b):
    t, d = h.shape
    tm = min(FINAL_TM, t)
    n = t // tm
    idx = dest.reshape(n, 1, tm * TOP_K)
    idx_spec = lambda fn: pl.BlockSpec((1, 1, tm * TOP_K), fn, memory_space=pltpu.SMEM)
    rowt = pl.BlockSpec((tm, d), lambda i: (i, 0))
    vec = pl.BlockSpec((1, d), lambda i: (0, 0))
    return pl.pallas_call(
        functools.partial(_final_kernel, tm=tm),
        grid=(n,),
        in_specs=[idx_spec(lambda i: (i, 0, 0)),
                  idx_spec(lambda i: (jnp.minimum(i + 1, n - 1), 0, 0)),
                  pl.BlockSpec(memory_space=pl.ANY),
                  rowt, rowt,
                  pl.BlockSpec((tm, LANES), lambda i: (i, 0)),
                  vec, vec],
        out_specs=rowt,
        out_shape=jax.ShapeDtypeStruct((t, d), F32),
        scratch_shapes=[pltpu.VMEM((2, TOP_K, tm, d // 2), jnp.uint32),
                        pltpu.SemaphoreType.DMA((2,))],
        compiler_params=_params(1),
        name="combine_ln2",
    )(idx, idx, ys, h, shared, top_w, ln_w, ln_b)


def _rotary_tables(positions):
    pos = positions.reshape(-1).astype(F32)[:, None]
    ret_freqs = 1.0 / (RET_THETA ** jnp.linspace(0.0, 1.0, RET_QK_DIM // 2, dtype=F32))
    ang = pos * ret_freqs
    ret_rot = (jnp.cos(ang), jnp.sin(ang))
    rope_freqs = ROPE_THETA ** (-jnp.arange(0, DIFF_QK_DIM, 2, dtype=F32) / DIFF_QK_DIM)
    ang = pos * rope_freqs
    cos, sin = jnp.cos(ang), jnp.sin(ang)
    diff_rot = (jnp.concatenate([cos, cos], axis=-1), jnp.concatenate([-sin, sin], axis=-1))
    return ret_rot, diff_rot


def _layer(h, positions, w_in, w_ret_proj, w_diff_proj, w_out, lq1, lk1, lq2, lk2, diff_norm_w,
           ln1_w, ln1_b, w_router, router_bias, exp_gate, exp_up, exp_down,
           shared_gate, shared_up, shared_down, ln2_w, ln2_b, lambda_init):
    batch, seq, d = h.shape
    t = batch * seq
    x = h.reshape(t, d)
    xb = x.astype(BF16)
    ret_rot, diff_rot = _rotary_tables(positions)

    off = 0
    rq = _proj(xb, w_in, off, RET_QW, "ret_rot", 1.0, ret_rot); off += RET_QW
    rk = _proj(xb, w_in, off, RET_QW, "ret_rot", RET_QK_DIM ** -0.5, ret_rot); off += RET_QW
    rv = _proj(xb, w_in, off, RET_VW, "plain"); off += RET_VW
    rg = _proj(xb, w_in, off, RET_VW, "silu"); off += RET_VW
    dq = _proj(xb, w_in, off, DIFF_QW, "diff_rot", LOG2_E * DIFF_QK_DIM ** -0.5, diff_rot)
    off += DIFF_QW
    dk = _proj(xb, w_in, off, DIFF_QW, "diff_rot", 1.0, diff_rot); off += DIFF_QW
    dv = _proj(xb, w_in, off, DIFF_VW, "plain"); off += DIFF_VW
    gate_ret = _proj(xb, w_in, off, d, "sigmoid"); off += d
    gate_diff = _proj(xb, w_in, off, d, "sigmoid")

    ret = _retention(rq, rk, rv, rg, batch, seq)
    row = lambda v: v.reshape(1, -1).astype(F32)
    da = _diff_attention(dq, dk, dv, row(lq1), row(lk1), row(lq2), row(lk2), row(diff_norm_w),
                         batch, seq, lambda_init)
    merged = _merge(ret, da, w_ret_proj.astype(BF16), w_diff_proj.astype(BF16),
                    gate_ret, gate_diff)
    y1 = _outproj(merged, w_out, x)
    h1, top_e, top_w, rank, counts = _ln_route(y1, row(ln1_w), row(ln1_b), w_router,
                                               row(router_bias))

    dest, blocks, zero_start, zero_cnt = _expert_plan(
        top_e[:, :TOP_K], rank[:, :TOP_K], counts, t)
    xs, shared = _dispatch_shared(h1, dest, zero_start, zero_cnt, blocks[0].shape[0],
                                  shared_gate.astype(BF16), shared_up.astype(BF16),
                                  shared_down.astype(BF16))
    ys = _routed_experts(xs, blocks, exp_gate, exp_up, exp_down)
    out = _final(ys, dest, h1, shared, top_w, row(ln2_w), row(ln2_b))
    return out.reshape(batch, seq, d)


def kernel(x, positions, w_in, w_ret_proj, w_diff_proj, w_out, lambda_q1, lambda_k1, lambda_q2,
           lambda_k2, diff_norm_w, ln1_w, ln1_b, w_router, router_bias, exp_gate, exp_up,
           exp_down, shared_gate, shared_up, shared_down, ln2_w, ln2_b):
    h = x
    for l in range(w_in.shape[0]):
        lambda_init = 0.8 - 0.6 * math.exp(-0.3 * l)
        h = _layer(h, positions, w_in[l], w_ret_proj[l], w_diff_proj[l], w_out[l],
                   lambda_q1[l], lambda_k1[l], lambda_q2[l], lambda_k2[l], diff_norm_w[l],
                   ln1_w[l], ln1_b[l], w_router[l], router_bias[l], exp_gate[l], exp_up[l],
                   exp_down[l], shared_gate[l], shared_up[l], shared_down[l],
                   ln2_w[l], ln2_b[l], lambda_init)
    return h
```

```python
import functools
import math

import jax
import jax.numpy as jnp
from jax import lax
from jax.experimental import pallas as pl
from jax.experimental.pallas import tpu as pltpu

F32 = jnp.float32
BF16 = jnp.bfloat16

D_MODEL = 4096
CHUNK = 64
RET_HEADS = 8
RET_QK_DIM = 256
RET_V_DIM = 512
RET_THETA = 10000.0
DIFF_HEADS = 16
DIFF_QK_DIM = 128
DIFF_V_DIM = 256
ROPE_THETA = 10000.0
RET_QW = RET_HEADS * RET_QK_DIM
RET_VW = RET_HEADS * RET_V_DIM
DIFF_QW = DIFF_HEADS * 2 * DIFF_QK_DIM
DIFF_VW = DIFF_HEADS * DIFF_V_DIM
N_EXPERTS = 64
TOP_K = 8
N_GROUPS = 8
TOPK_GROUPS = 4
EXPERT_DIM = 512
ROUTED_SCALE = 2.5
DEPTH = 1
DN_ALPHA = (2.0 * DEPTH) ** 0.25
EPS = 1e-5
LOG2_E = math.log2(math.e)

LANES = 128
SUBLANES = 8
MIB = 1024 * 1024
VMEM_LIMIT = 56 * MIB
EXPERT_VMEM_LIMIT = 60 * MIB

PROJ_TM, PROJ_TN = 1024, 512
MERGE_TM, MERGE_TN = 512, 512
RET_BLOCK = 256
RET_HEADS_PER_STEP = 2
ATT_BLOCK = 512
ATT_KEY_WIDTH = 1024
LN_TM = 256
MOE_BLOCK = 256
DISPATCH_TM = 128
FINAL_TM = 128


def _params(n_axes, vmem_limit=VMEM_LIMIT):
    return pltpu.CompilerParams(dimension_semantics=("arbitrary",) * n_axes,
                                vmem_limit_bytes=vmem_limit)


def _sigmoid(v):
    return 1.0 / (1.0 + jnp.exp(-v))


def _proj_kernel(*refs, mode, scale, tn):
    if mode in ("ret_rot", "diff_rot"):
        x_ref, w_ref, c_ref, s_ref, o_ref = refs
    else:
        x_ref, w_ref, o_ref = refs
    acc = jnp.dot(x_ref[...], w_ref[...].astype(BF16), preferred_element_type=F32)
    if mode == "ret_rot":
        cos, sin = c_ref[...], s_ref[...]
        for h in range(tn // RET_QK_DIM):
            lo = h * RET_QK_DIM
            x1 = acc[:, lo:lo + LANES]
            x2 = acc[:, lo + LANES:lo + 2 * LANES]
            o_ref[:, lo:lo + LANES] = ((x1 * cos - x2 * sin) * scale).astype(o_ref.dtype)
            o_ref[:, lo + LANES:lo + 2 * LANES] = ((x2 * cos + x1 * sin) * scale).astype(o_ref.dtype)
    elif mode == "diff_rot":
        c, s = c_ref[...], s_ref[...]
        for g in range(tn // LANES):
            xg = acc[:, g * LANES:(g + 1) * LANES]
            rot = pltpu.roll(xg, LANES // 2, axis=1)
            o_ref[:, g * LANES:(g + 1) * LANES] = ((xg * c + rot * s) * scale).astype(o_ref.dtype)
    elif mode == "silu":
        o_ref[...] = (acc * _sigmoid(acc)).astype(o_ref.dtype)
    elif mode == "sigmoid":
        o_ref[...] = _sigmoid(acc).astype(o_ref.dtype)
    else:
        o_ref[...] = acc.astype(o_ref.dtype)


def _proj(xb, w, col_off, width, mode, scale=1.0, rot=None):
    t, d = xb.shape
    tm, tn = min(PROJ_TM, t), PROJ_TN
    joff = col_off // tn
    in_specs = [pl.BlockSpec((tm, d), lambda i, j: (i, 0)),
                pl.BlockSpec((d, tn), lambda i, j: (0, j + joff))]
    args = [xb, w]
    if rot is not None:
        in_specs += [pl.BlockSpec((tm, LANES), lambda i, j: (i, 0))] * 2
        args += list(rot)
    return pl.pallas_call(
        functools.partial(_proj_kernel, mode=mode, scale=scale, tn=tn),
        grid=(t // tm, width // tn),
        in_specs=in_specs,
        out_specs=pl.BlockSpec((tm, tn), lambda i, j: (i, j)),
        out_shape=jax.ShapeDtypeStruct((t, width), BF16),
        compiler_params=_params(2),
        name="proj_" + mode,
    )(*args)


def _ret_kernel(q_ref, k_ref, v_ref, g_ref, dm_ref, qd_ref, kd_ref, cd_ref, o_ref, r_ref):
    @pl.when(pl.program_id(2) == 0)
    def _():
        r_ref[...] = jnp.zeros_like(r_ref)

    for hh in range(RET_HEADS_PER_STEP):
        qk = slice(hh * RET_QK_DIM, (hh + 1) * RET_QK_DIM)
        vv = slice(hh * RET_V_DIM, (hh + 1) * RET_V_DIM)
        q, k, v = q_ref[:, qk], k_ref[:, qk], v_ref[:, vv]
        s = lax.dot_general(q, k, (((1,), (1,)), ((), ())), preferred_element_type=F32)
        s = (s * dm_ref[hh]).astype(BF16)
        inner = jnp.dot(s, v, preferred_element_type=F32)
        qs = (q.astype(F32) * qd_ref[hh]).astype(BF16)
        r = r_ref[hh]
        cross = jnp.dot(qs, r.astype(BF16), preferred_element_type=F32)
        ks = (k.astype(F32) * kd_ref[hh]).astype(BF16)
        r_ref[hh] = r * cd_ref[hh] + lax.dot_general(
            ks, v, (((0,), (0,)), ((), ())), preferred_element_type=F32)
        y = inner + cross
        mu = jnp.mean(y, axis=-1, keepdims=True)
        yc = y - mu
        var = jnp.mean(yc * yc, axis=-1, keepdims=True)
        o_ref[:, vv] = (yc * lax.rsqrt(var + EPS) * g_ref[:, vv].astype(F32)).astype(o_ref.dtype)


def _retention_tables(blk):
    log_g = jnp.log1p(-jnp.exp2(-5.0 - jnp.arange(RET_HEADS, dtype=F32)))
    idx = jnp.arange(blk, dtype=F32)
    chunk = jnp.arange(blk) // CHUNK
    visible = chunk[None, :] <= chunk[:, None]
    dm = jnp.where(visible[None],
                   jnp.exp(log_g[:, None, None] * jnp.abs(idx[:, None] - idx[None, :])), 0.0)
    qd = jnp.exp(log_g[:, None] * (idx + 1.0))
    kd = jnp.exp(log_g[:, None] * (blk - 1.0 - idx))
    cd = jnp.exp(log_g * blk)
    qd = jnp.broadcast_to(qd[:, :, None], (RET_HEADS, blk, RET_QK_DIM))
    kd = jnp.broadcast_to(kd[:, :, None], (RET_HEADS, blk, RET_QK_DIM))
    cd = jnp.broadcast_to(cd[:, None, None], (RET_HEADS, 1, RET_V_DIM))
    return dm, qd, kd, cd


def _retention(rq, rk, rv, rg, batch, seq):
    t = batch * seq
    blk = min(RET_BLOCK, seq)
    nl = seq // blk
    dm, qd, kd, cd = _retention_tables(blk)
    hs = RET_HEADS_PER_STEP
    row = lambda b, h, l: (b * nl + l, h)
    head = lambda b, h, l: (h, 0, 0)
    return pl.pallas_call(
        _ret_kernel,
        grid=(batch, RET_HEADS // hs, nl),
        in_specs=[pl.BlockSpec((blk, hs * RET_QK_DIM), row),
                  pl.BlockSpec((blk, hs * RET_QK_DIM), row),
                  pl.BlockSpec((blk, hs * RET_V_DIM), row),
                  pl.BlockSpec((blk, hs * RET_V_DIM), row),
                  pl.BlockSpec((hs, blk, blk), head),
                  pl.BlockSpec((hs, blk, RET_QK_DIM), head),
                  pl.BlockSpec((hs, blk, RET_QK_DIM), head),
                  pl.BlockSpec((hs, 1, RET_V_DIM), head)],
        out_specs=pl.BlockSpec((blk, hs * RET_V_DIM), row),
        out_shape=jax.ShapeDtypeStruct((t, RET_VW), BF16),
        scratch_shapes=[pltpu.VMEM((hs, RET_QK_DIM, RET_V_DIM), F32)],
        compiler_params=_params(3),
        name="retention",
    )(rq, rk, rv, rg, dm, qd, kd, cd)


def _attn_kernel(lq1_ref, lk1_ref, lq2_ref, lk2_ref, nw_ref, q_ref, k_ref, v_ref, o_ref,
                 m_ref, l_ref, acc_ref, *, blk, lambda_init):
    qi = pl.program_id(2)
    m_ref[...] = jnp.full_like(m_ref, -jnp.inf)
    l_ref[...] = jnp.zeros_like(l_ref)
    acc_ref[...] = jnp.zeros_like(acc_ref)
    q = q_ref[...]

    def block(first_key, width, row0=0, rows=blk, diag_key0=None):
        off = pl.multiple_of(first_key, width)
        kb = k_ref[pl.ds(off, width), :]
        vb = v_ref[pl.ds(off, width), :]
        rs = slice(row0, row0 + rows)
        if diag_key0 is not None:
            rchunk = (row0 + lax.broadcasted_iota(jnp.int32, (rows, width), 0)) // CHUNK
            cchunk = (diag_key0 + lax.broadcasted_iota(jnp.int32, (rows, width), 1)) // CHUNK
            visible = cchunk <= rchunk
        for c in range(2):
            qc = q[rs, c * DIFF_QK_DIM:(c + 1) * DIFF_QK_DIM]
            kc = kb[:, c * DIFF_QK_DIM:(c + 1) * DIFF_QK_DIM]
            s = lax.dot_general(qc, kc, (((1,), (1,)), ((), ())), preferred_element_type=F32)
            if diag_key0 is not None:
                s = jnp.where(visible, s, -jnp.inf)
            m_prev = m_ref[c, rs]
            m_new = jnp.maximum(m_prev, jnp.max(s, axis=-1, keepdims=True))
            p = jnp.exp2(s - jnp.tile(m_new, (1, width // LANES)))
            alpha = jnp.exp2(m_prev - m_new)
            l_ref[c, rs] = alpha * l_ref[c, rs] + jnp.sum(p, axis=-1, keepdims=True)
            acc_ref[c, rs] = (jnp.tile(alpha, (1, DIFF_V_DIM // LANES)) * acc_ref[c, rs]
                              + jnp.dot(p.astype(BF16), vb, preferred_element_type=F32))
            m_ref[c, rs] = m_new

    per = max(ATT_KEY_WIDTH // blk, 1)

    def body(j, carry):
        block(j * (per * blk), per * blk)
        return carry

    lax.fori_loop(0, qi // per, body, 0)

    if per == 2:
        @pl.when(lax.rem(qi, 2) == 1)
        def _():
            block((qi - 1) * blk, blk)

    half = blk // 2
    block(qi * blk, half, diag_key0=0)
    block(qi * blk + half, half, row0=half, rows=half, diag_key0=half)

    lam = (jnp.exp(jnp.sum(lq1_ref[...] * lk1_ref[...], axis=-1, keepdims=True))
           - jnp.exp(jnp.sum(lq2_ref[...] * lk2_ref[...], axis=-1, keepdims=True))
           + lambda_init)
    rep = DIFF_V_DIM // LANES
    o = (acc_ref[0] / jnp.tile(l_ref[0], (1, rep))
         - lam * (acc_ref[1] / jnp.tile(l_ref[1], (1, rep))))
    ms = jnp.mean(o * o, axis=-1, keepdims=True)
    o_ref[...] = (o * lax.rsqrt(ms + EPS) * nw_ref[...] * (1.0 - lambda_init)).astype(o_ref.dtype)


def _diff_attention(dq, dk, dv, lq1, lk1, lq2, lk2, norm_w, batch, seq, lambda_init):
    t = batch * seq
    blk = min(ATT_BLOCK, seq)
    nq = seq // blk
    hw = 2 * DIFF_QK_DIM
    vec = lambda b, h, i: (0, 0)
    return pl.pallas_call(
        functools.partial(_attn_kernel, blk=blk, lambda_init=lambda_init),
        grid=(batch, DIFF_HEADS, nq),
        in_specs=[pl.BlockSpec((1, DIFF_QK_DIM), vec)] * 4
        + [pl.BlockSpec((1, DIFF_V_DIM), vec),
           pl.BlockSpec((blk, hw), lambda b, h, i: (b * nq + i, h)),
           pl.BlockSpec((seq, hw), lambda b, h, i: (b, h)),
           pl.BlockSpec((seq, DIFF_V_DIM), lambda b, h, i: (b, h))],
        out_specs=pl.BlockSpec((blk, DIFF_V_DIM), lambda b, h, i: (b * nq + i, h)),
        out_shape=jax.ShapeDtypeStruct((t, DIFF_VW), BF16),
        scratch_shapes=[pltpu.VMEM((2, blk, LANES), F32),
                        pltpu.VMEM((2, blk, LANES), F32),
                        pltpu.VMEM((2, blk, DIFF_V_DIM), F32)],
        compiler_params=_params(3),
        name="diff_attention",
    )(lq1, lk1, lq2, lk2, norm_w, dq, dk, dv)


def _merge_kernel(r_ref, d_ref, wr_ref, wd_ref, gr_ref, gd_ref, o_ref):
    a = jnp.dot(r_ref[...], wr_ref[...], preferred_element_type=F32)
    b = jnp.dot(d_ref[...], wd_ref[...], preferred_element_type=F32)
    o_ref[...] = (gr_ref[...].astype(F32) * a + gd_ref[...].astype(F32) * b).astype(o_ref.dtype)


def _merge(ret, da, wr, wd, gr, gd):
    t, kdim = ret.shape
    n = wr.shape[1]
    tm, tn = min(MERGE_TM, t), MERGE_TN
    lhs = pl.BlockSpec((tm, kdim), lambda i, j: (i, 0))
    rhs = pl.BlockSpec((kdim, tn), lambda i, j: (0, j))
    tile = pl.BlockSpec((tm, tn), lambda i, j: (i, j))
    return pl.pallas_call(
        _merge_kernel,
        grid=(t // tm, n // tn),
        in_specs=[lhs, lhs, rhs, rhs, tile, tile],
        out_specs=tile,
        out_shape=jax.ShapeDtypeStruct((t, n), BF16),
        compiler_params=_params(2),
        name="merge",
    )(ret, da, wr, wd, gr, gd)


def _outproj_kernel(m_ref, w_ref, x_ref, o_ref):
    o_ref[...] = DN_ALPHA * x_ref[...] + jnp.dot(m_ref[...], w_ref[...].astype(BF16),
                                                  preferred_element_type=F32)


def _outproj(merged, w, x):
    t, kdim = merged.shape
    n = w.shape[1]
    tm, tn = min(PROJ_TM, t), PROJ_TN
    tile = pl.BlockSpec((tm, tn), lambda i, j: (i, j))
    return pl.pallas_call(
        _outproj_kernel,
        grid=(t // tm, n // tn),
        in_specs=[pl.BlockSpec((tm, kdim), lambda i, j: (i, 0)),
                  pl.BlockSpec((kdim, tn), lambda i, j: (0, j)),
                  tile],
        out_specs=tile,
        out_shape=jax.ShapeDtypeStruct((t, n), F32),
        compiler_params=_params(2),
        name="outproj",
    )(merged, w, x)


def _layernorm(y, w, b):
    mu = jnp.mean(y, axis=-1, keepdims=True)
    yc = y - mu
    var = jnp.mean(yc * yc, axis=-1, keepdims=True)
    return yc * lax.rsqrt(var + EPS) * w + b


def _ln_route_kernel(y_ref, lw_ref, lb_ref, wr_ref, rb_ref, h_ref, te_ref, tw_ref, tr_ref, tc_ref,
                     cnt_ref):
    h = _layernorm(y_ref[...], lw_ref[...], lb_ref[...])
    h_ref[...] = h
    hi = h.astype(BF16)
    lo = (h - hi.astype(F32)).astype(BF16)
    w = wr_ref[...]
    whi = w.astype(BF16)
    wlo = (w - whi.astype(F32)).astype(BF16)
    logits = (jnp.dot(hi, whi, preferred_element_type=F32)
              + (jnp.dot(hi, wlo, preferred_element_type=F32)
                 + jnp.dot(lo, whi, preferred_element_type=F32)))
    scores = _sigmoid(logits)
    biased = scores + rb_ref[...]
    tm = scores.shape[0]
    neg = -jnp.inf
    lane_i = lax.broadcasted_iota(jnp.int32, (tm, N_EXPERTS), 1)
    lane = lane_i.astype(F32)
    per_group = N_EXPERTS // N_GROUPS
    grp = lane_i // per_group

    gscores = []
    gs_lane = jnp.zeros((tm, N_EXPERTS), F32)
    for g in range(N_GROUPS):
        mk = grp == g
        vals = jnp.where(mk, biased, neg)
        m1 = jnp.max(vals, axis=-1, keepdims=True)
        i1 = jnp.min(jnp.where(vals == m1, lane, float(N_EXPERTS)), axis=-1, keepdims=True)
        m2 = jnp.max(jnp.where(lane == i1, neg, vals), axis=-1, keepdims=True)
        gscores.append(m1 + m2)
        gs_lane = jnp.where(mk, m1 + m2, gs_lane)
    beaten = jnp.zeros((tm, N_EXPERTS), F32)
    for g in range(N_GROUPS):
        wins = jnp.where(gscores[g] > gs_lane, 1.0,
                         jnp.where(gscores[g] == gs_lane,
                                   jnp.where(grp > g, 1.0, 0.0), 0.0))
        beaten = beaten + wins
    cur = jnp.where(beaten < float(TOPK_GROUPS), biased, neg)

    out_lane = lax.broadcasted_iota(jnp.int32, (tm, LANES), 1)
    sel_e = jnp.zeros((tm, LANES), F32)
    sel_w = jnp.zeros((tm, LANES), F32)
    wsum = jnp.zeros((tm, 1), F32)
    hits = []
    for k in range(TOP_K):
        m = jnp.max(cur, axis=-1, keepdims=True)
        idx = jnp.min(jnp.where(cur == m, lane, float(N_EXPERTS)), axis=-1, keepdims=True)
        hit = lane == idx
        wk = jnp.sum(jnp.where(hit, scores, 0.0), axis=-1, keepdims=True)
        cur = jnp.where(hit, neg, cur)
        wsum = wsum + wk
        sel_e = jnp.where(out_lane == k, idx, sel_e)
        sel_w = jnp.where(out_lane == k, wk, sel_w)
        hits.append(hit)
    te_ref[...] = sel_e.astype(jnp.int32)
    tw_ref[...] = sel_w / wsum * ROUTED_SCALE

    @pl.when(pl.program_id(0) == 0)
    def _():
        cnt_ref[...] = jnp.zeros_like(cnt_ref)

    chosen = jnp.zeros((tm, N_EXPERTS), F32)
    for hit in hits:
        chosen = chosen + jnp.where(hit, 1.0, 0.0)
    earlier = (lax.broadcasted_iota(jnp.int32, (tm, tm), 0)
               > lax.broadcasted_iota(jnp.int32, (tm, tm), 1))
    prefix = jnp.dot(jnp.where(earlier, 1.0, 0.0).astype(BF16), chosen.astype(BF16),
                     preferred_element_type=F32) + cnt_ref[...]
    sel_r = jnp.zeros((tm, LANES), F32)
    for k, hit in enumerate(hits):
        rk = jnp.sum(jnp.where(hit, prefix, 0.0), axis=-1, keepdims=True)
        sel_r = jnp.where(out_lane == k, rk, sel_r)
    tr_ref[...] = sel_r.astype(jnp.int32)
    cnt_ref[...] = cnt_ref[...] + jnp.sum(chosen, axis=0, keepdims=True)
    tc_ref[...] = cnt_ref[...]


def _ln_route(y, ln_w, ln_b, w_router, router_bias):
    t, d = y.shape
    tm = min(LN_TM, t)
    rowt = pl.BlockSpec((tm, d), lambda i: (i, 0))
    vec = pl.BlockSpec((1, d), lambda i: (0, 0))
    narrow = pl.BlockSpec((tm, LANES), lambda i: (i, 0))
    return pl.pallas_call(
        _ln_route_kernel,
        grid=(t // tm,),
        in_specs=[rowt, vec, vec,
                  pl.BlockSpec((d, N_EXPERTS), lambda i: (0, 0)),
                  pl.BlockSpec((1, N_EXPERTS), lambda i: (0, 0))],
        out_specs=[rowt, narrow, narrow, narrow,
                   pl.BlockSpec((1, N_EXPERTS), lambda i: (0, 0))],
        out_shape=[jax.ShapeDtypeStruct((t, d), F32),
                   jax.ShapeDtypeStruct((t, LANES), jnp.int32),
                   jax.ShapeDtypeStruct((t, LANES), F32),
                   jax.ShapeDtypeStruct((t, LANES), jnp.int32),
                   jax.ShapeDtypeStruct((1, N_EXPERTS), F32)],
        scratch_shapes=[pltpu.VMEM((1, N_EXPERTS), F32)],
        compiler_params=_params(1),
        name="ln_route",
    )(y, ln_w, ln_b, w_router, router_bias)


def _expert_plan(top_e, rank, counts, t):
    nb = t * TOP_K // MOE_BLOCK + N_EXPERTS
    counts = counts.reshape(N_EXPERTS).astype(jnp.int32)
    padded = (counts + MOE_BLOCK - 1) // MOE_BLOCK * MOE_BLOCK
    pad_end = jnp.cumsum(padded)
    pad_start = pad_end - padded
    experts = jnp.arange(N_EXPERTS, dtype=jnp.int32)
    dest = jnp.sum(jnp.where(top_e[..., None] == experts, pad_start, 0), axis=-1) + rank
    n_active = (pad_end[-1] // MOE_BLOCK).astype(jnp.int32)
    blk_start = jnp.arange(nb, dtype=jnp.int32) * MOE_BLOCK
    block_e = jnp.sum((pad_end[None, :] <= blk_start[:, None]).astype(jnp.int32), axis=-1)
    block_e = jnp.minimum(block_e, N_EXPERTS - 1)
    last_e = jnp.sum((pad_end <= (n_active - 1) * MOE_BLOCK).astype(jnp.int32))
    block_e = jnp.where(jnp.arange(nb) < n_active, block_e, jnp.minimum(last_e, N_EXPERTS - 1))
    first = jnp.concatenate([jnp.ones((1,), jnp.int32),
                             (block_e[1:] != block_e[:-1]).astype(jnp.int32)])
    later = (experts[None, :] > experts[:, None]) & (counts[None, :] > 0)
    next_e = jnp.min(jnp.where(later, experts[None, :], N_EXPERTS), axis=-1)
    next_e = jnp.where(next_e == N_EXPERTS, experts, next_e)
    block_next = jnp.sum(jnp.where(block_e[:, None] == experts[None, :], next_e[None, :], 0),
                         axis=-1)
    tail_pieces = (nb * MOE_BLOCK - pad_end[-1]) // (MOE_BLOCK // 2)
    zero_start = jnp.concatenate([pad_start + counts, pad_end[-1:]]).astype(jnp.int32)
    zero_cnt = jnp.concatenate([padded - counts, tail_pieces[None]]).astype(jnp.int32)
    blocks = (block_e.astype(jnp.int32), block_next.astype(jnp.int32), first,
              n_active.reshape(1))
    return dest.astype(jnp.int32), blocks, zero_start, zero_cnt


def _pack_halves(v):
    half = v.shape[-1] // 2
    bits = lambda a: lax.bitcast_convert_type(a.astype(BF16).astype(F32), jnp.uint32)
    return (bits(v[:, :half]) >> 16) | (bits(v[:, half:]) & jnp.uint32(0xFFFF0000))


def _unpack_halves(w):
    lo = lax.bitcast_convert_type(w << 16, F32)
    hi = lax.bitcast_convert_type(w & jnp.uint32(0xFFFF0000), F32)
    return lo, hi


def _dispatch_kernel(zs_ref, zn_ref, dest_ref, h_ref, wg_ref, wu_ref, wd_ref, xs_hbm, s_ref,
                     buf, zbuf, sem, zsem, *, tm):
    i = pl.program_id(0)
    n = pl.num_programs(0)
    slot = lax.rem(i, 2)

    def zero_rows(wait):
        def zero_copy(off, rows):
            cp = pltpu.make_async_copy(zbuf.at[pl.ds(0, rows)], xs_hbm.at[pl.ds(off, rows)], zsem)
            if wait:
                cp.wait()
            else:
                cp.start()

        def single_rows(off, count):
            def one(j, carry):
                zero_copy(off + j, 1)
                return carry
            lax.fori_loop(0, count, one, 0)

        def body(e, carry):
            cnt = zn_ref[e]
            start = zs_ref[e]
            head = jnp.minimum((-start) & (SUBLANES - 1), cnt)
            single_rows(start, head)
            rest = cnt - head
            off = start + head
            rows = MOE_BLOCK // 2
            while rows >= SUBLANES:
                @pl.when((rest & rows) != 0)
                def _():
                    zero_copy(pl.multiple_of(off, SUBLANES), rows)
                off = off + (rest & rows)
                rows //= 2
            single_rows(off, rest & (SUBLANES - 1))
            return carry
        lax.fori_loop(0, N_EXPERTS, body, 0)

        def tail(c, carry):
            zero_copy(pl.multiple_of(zs_ref[N_EXPERTS] + c * (MOE_BLOCK // 2), MOE_BLOCK // 2),
                      MOE_BLOCK // 2)
            return carry
        lax.fori_loop(0, zn_ref[N_EXPERTS], tail, 0)

    @pl.when(i == 0)
    def _():
        zbuf[...] = jnp.zeros_like(zbuf)
        zero_rows(False)
        zero_rows(True)

    def wait_rows(s):
        for _ in range(TOP_K):
            pltpu.make_async_copy(buf.at[s], xs_hbm.at[pl.ds(0, tm)], sem.at[s]).wait()

    @pl.when(i >= 2)
    def _():
        wait_rows(slot)

    h = h_ref[...]
    buf[slot] = _pack_halves(h)
    for r in range(tm):
        for k in range(TOP_K):
            pltpu.make_async_copy(buf.at[slot, pl.ds(r, 1)],
                                  xs_hbm.at[pl.ds(dest_ref[0, 0, r * TOP_K + k], 1)],
                                  sem.at[slot]).start()

    x = h.astype(BF16)
    g = jnp.dot(x, wg_ref[...], preferred_element_type=F32)
    u = jnp.dot(x, wu_ref[...], preferred_element_type=F32)
    act = (g * _sigmoid(g) * u).astype(BF16)
    s_ref[...] = jnp.dot(act, wd_ref[...], preferred_element_type=F32)

    @pl.when(i == n - 1)
    def _():
        wait_rows(slot)

    @pl.when((i == n - 1) & (n >= 2))
    def _():
        wait_rows(1 - slot)


def _dispatch_shared(h, dest, zero_start, zero_cnt, nb, wg, wu, wd):
    t, d = h.shape
    e_dim = wg.shape[-1]
    tm = min(DISPATCH_TM, t)
    rowt = pl.BlockSpec((tm, d), lambda i, zs, zn: (i, 0))
    whole = lambda shape: pl.BlockSpec(shape, lambda i, zs, zn: (0, 0))
    grid_spec = pltpu.PrefetchScalarGridSpec(
        num_scalar_prefetch=2,
        grid=(t // tm,),
        in_specs=[pl.BlockSpec((1, 1, tm * TOP_K), lambda i, zs, zn: (i, 0, 0),
                               memory_space=pltpu.SMEM),
                  rowt, whole((d, e_dim)), whole((d, e_dim)), whole((e_dim, d))],
        out_specs=[pl.BlockSpec(memory_space=pl.ANY), rowt],
        scratch_shapes=[pltpu.VMEM((2, tm, d // 2), jnp.uint32),
                        pltpu.VMEM((MOE_BLOCK // 2, d // 2), jnp.uint32),
                        pltpu.SemaphoreType.DMA((2,)),
                        pltpu.SemaphoreType.DMA(())],
    )
    return pl.pallas_call(
        functools.partial(_dispatch_kernel, tm=tm),
        grid_spec=grid_spec,
        out_shape=[jax.ShapeDtypeStruct((nb * MOE_BLOCK, d // 2), jnp.uint32),
                   jax.ShapeDtypeStruct((t, d), F32)],
        compiler_params=_params(1),
        name="dispatch_shared",
    )(zero_start, zero_cnt, dest.reshape(t // tm, 1, tm * TOP_K), h, wg, wu, wd)


def _expert_kernel(be_ref, nx_ref, first_ref, na_ref, x_ref, wg_hbm, wu_hbm, wd_hbm, o_ref,
                   sg, su, sd, wg_ref, wu_ref, wd_ref, sem):
    i = pl.program_id(0)
    staged = ((wg_hbm, sg, wg_ref), (wu_hbm, su, wu_ref), (wd_hbm, sd, wd_ref))

    def staging_copy(s, e):
        src, stage, _ = staged[s]
        return pltpu.make_async_copy(src.at[e], stage, sem.at[s])

    def land():
        for s in range(len(staged)):
            staging_copy(s, 0).wait()

    @pl.when(i < na_ref[0])
    def _():
        @pl.when(i == 0)
        def _():
            for s in range(len(staged)):
                staging_copy(s, be_ref[0]).start()

        def compute(first):
            def working(s):
                _, stage, work = staged[s]
                if first:
                    work[...] = stage[...].astype(BF16)
                    staging_copy(s, nx_ref[i]).start()
                return work

            if first:
                land()
            lo, hi = _unpack_halves(x_ref[...])
            lo, hi = lo.astype(BF16), hi.astype(BF16)
            half = lo.shape[-1]

            def in_proj(w_ref):
                return (jnp.dot(lo, w_ref[:half, :], preferred_element_type=F32)
                        + jnp.dot(hi, w_ref[half:, :], preferred_element_type=F32))

            g = in_proj(working(0))
            u = in_proj(working(1))
            act = (g * _sigmoid(g) * u).astype(BF16)
            o_ref[...] = _pack_halves(jnp.dot(act, working(2)[...], preferred_element_type=F32))

        @pl.when(first_ref[i] == 1)
        def _():
            compute(True)

        @pl.when(first_ref[i] != 1)
        def _():
            compute(False)

        @pl.when(i == na_ref[0] - 1)
        def _():
            land()

    @pl.when(i >= na_ref[0])
    def _():
        o_ref[...] = jnp.zeros_like(o_ref)


def _routed_experts(xs, blocks, wg, wu, wd):
    p, dh = xs.shape
    nb = p // MOE_BLOCK
    d, e_dim = wg.shape[-2:]
    anywhere = pl.BlockSpec(memory_space=pl.ANY)
    grid_spec = pltpu.PrefetchScalarGridSpec(
        num_scalar_prefetch=4,
        grid=(nb,),
        in_specs=[pl.BlockSpec((MOE_BLOCK, dh),
                               lambda i, be, nx, ff, na:
                               (jnp.minimum(i, jnp.maximum(na[0] - 1, 0)), 0)),
                  anywhere, anywhere, anywhere],
        out_specs=pl.BlockSpec((MOE_BLOCK, dh), lambda i, be, nx, ff, na: (i, 0)),
        scratch_shapes=[pltpu.VMEM((d, e_dim), F32), pltpu.VMEM((d, e_dim), F32),
                        pltpu.VMEM((e_dim, d), F32),
                        pltpu.VMEM((d, e_dim), BF16), pltpu.VMEM((d, e_dim), BF16),
                        pltpu.VMEM((e_dim, d), BF16),
                        pltpu.SemaphoreType.DMA((3,))],
    )
    return pl.pallas_call(
        _expert_kernel,
        grid_spec=grid_spec,
        out_shape=jax.ShapeDtypeStruct((p, dh), jnp.uint32),
        compiler_params=_params(1, EXPERT_VMEM_LIMIT),
        name="routed_experts",
    )(*blocks, xs, wg, wu, wd)


def _final_kernel(dest_ref, destn_ref, ys_hbm, h_ref, s_ref, tw_ref, lw_ref, lb_ref, o_ref,
                  gbuf, sem, *, tm):
    i = pl.program_id(0)
    n = pl.num_programs(0)
    slot = lax.rem(i, 2)

    def row_copy(idx_ref, s, r, k):
        return pltpu.make_async_copy(ys_hbm.at[pl.ds(idx_ref[0, 0, r * TOP_K + k], 1)],
                                     gbuf.at[s, k, pl.ds(r, 1)], sem.at[s])

    def gather_wait(s):
        for k in range(TOP_K):
            pltpu.make_async_copy(ys_hbm.at[pl.ds(0, tm)], gbuf.at[s, k], sem.at[s]).wait()

    @pl.when(i == 0)
    def _():
        def body(r, carry):
            for k in range(TOP_K):
                row_copy(dest_ref, 0, r, k).start()
            return carry
        lax.fori_loop(0, tm, body, 0)

    gather_wait(slot)
    for r in range(tm):
        for k in range(TOP_K):
            row_copy(destn_ref, 1 - slot, r, k).start()

    tw = tw_ref[...]
    half = gbuf.shape[-1]
    y_lo = DN_ALPHA * h_ref[:, :half] + s_ref[:, :half]
    y_hi = DN_ALPHA * h_ref[:, half:] + s_ref[:, half:]
    for k in range(TOP_K):
        lo, hi = _unpack_halves(gbuf[slot, k])
        y_lo = y_lo + tw[:, k:k + 1] * lo
        y_hi = y_hi + tw[:, k:k + 1] * hi
    inv_d = 1.0 / (2 * half)
    mu = (jnp.sum(y_lo, axis=-1, keepdims=True) + jnp.sum(y_hi, axis=-1, keepdims=True)) * inv_d
    c_lo, c_hi = y_lo - mu, y_hi - mu
    var = (jnp.sum(c_lo * c_lo, axis=-1, keepdims=True)
           + jnp.sum(c_hi * c_hi, axis=-1, keepdims=True)) * inv_d
    r = lax.rsqrt(var + EPS)
    o_ref[:, :half] = c_lo * r * lw_ref[:, :half] + lb_ref[:, :half]
    o_ref[:, half:] = c_hi * r * lw_ref[:, half:] + lb_ref[:, half:]

    @pl.when(i == n - 1)
    def _():
        gather_wait(1 - slot)


def _final(ys, dest, h, shared, top_w, ln_w, ln_b):
    t, d = h.shape
    tm = min(FINAL_TM, t)
    n = t // tm
    idx = dest.reshape(n, 1, tm * TOP_K)
    idx_spec = lambda fn: pl.BlockSpec((1, 1, tm * TOP_K), fn, memory_space=pltpu.SMEM)
    rowt = pl.BlockSpec((tm, d), lambda i: (i, 0))
    vec = pl.BlockSpec((1, d), lambda i: (0, 0))
    return pl.pallas_call(
        functools.partial(_final_kernel, tm=tm),
        grid=(n,),
        in_specs=[idx_spec(lambda i: (i, 0, 0)),
                  idx_spec(lambda i: (jnp.minimum(i + 1, n - 1), 0, 0)),
                  pl.BlockSpec(memory_space=pl.ANY),
                  rowt, rowt,
                  pl.BlockSpec((tm, LANES), lambda i: (i, 0)),
                  vec, vec],
        out_specs=rowt,
        out_shape=jax.ShapeDtypeStruct((t, d), F32),
        scratch_shapes=[pltpu.VMEM((2, TOP_K, tm, d // 2), jnp.uint32),
                        pltpu.SemaphoreType.DMA((2,))],
        compiler_params=_params(1),
        name="combine_ln2",
    )(idx, idx, ys, h, shared, top_w, ln_w, ln_b)


def _rotary_tables(positions):
    pos = positions.reshape(-1).astype(F32)[:, None]
    ret_freqs = 1.0 / (RET_THETA ** jnp.linspace(0.0, 1.0, RET_QK_DIM // 2, dtype=F32))
    ang = pos * ret_freqs
    ret_rot = (jnp.cos(ang), jnp.sin(ang))
    rope_freqs = ROPE_THETA ** (-jnp.arange(0, DIFF_QK_DIM, 2, dtype=F32) / DIFF_QK_DIM)
    ang = pos * rope_freqs
    cos, sin = jnp.cos(ang), jnp.sin(ang)
    diff_rot = (jnp.concatenate([cos, cos], axis=-1), jnp.concatenate([-sin, sin], axis=-1))
    return ret_rot, diff_rot


def _layer(h, positions, w_in, w_ret_proj, w_diff_proj, w_out, lq1, lk1, lq2, lk2, diff_norm_w,
           ln1_w, ln1_b, w_router, router_bias, exp_gate, exp_up, exp_down,
           shared_gate, shared_up, shared_down, ln2_w, ln2_b, lambda_init):
    batch, seq, d = h.shape
    t = batch * seq
    x = h.reshape(t, d)
    xb = x.astype(BF16)
    ret_rot, diff_rot = _rotary_tables(positions)

    off = 0
    rq = _proj(xb, w_in, off, RET_QW, "ret_rot", 1.0, ret_rot); off += RET_QW
    rk = _proj(xb, w_in, off, RET_QW, "ret_rot", RET_QK_DIM ** -0.5, ret_rot); off += RET_QW
    rv = _proj(xb, w_in, off, RET_VW, "plain"); off += RET_VW
    rg = _proj(xb, w_in, off, RET_VW, "silu"); off += RET_VW
    dq = _proj(xb, w_in, off, DIFF_QW, "diff_rot", LOG2_E * DIFF_QK_DIM ** -0.5, diff_rot)
    off += DIFF_QW
    dk = _proj(xb, w_in, off, DIFF_QW, "diff_rot", 1.0, diff_rot); off += DIFF_QW
    dv = _proj(xb, w_in, off, DIFF_VW, "plain"); off += DIFF_VW
    gate_ret = _proj(xb, w_in, off, d, "sigmoid"); off += d
    gate_diff = _proj(xb, w_in, off, d, "sigmoid")

    ret = _retention(rq, rk, rv, rg, batch, seq)
    row = lambda v: v.reshape(1, -1).astype(F32)
    da = _diff_attention(dq, dk, dv, row(lq1), row(lk1), row(lq2), row(lk2), row(diff_norm_w),
                         batch, seq, lambda_init)
    merged = _merge(ret, da, w_ret_proj.astype(BF16), w_diff_proj.astype(BF16),
                    gate_ret, gate_diff)
    y1 = _outproj(merged, w_out, x)
    h1, top_e, top_w, rank, counts = _ln_route(y1, row(ln1_w), row(ln1_b), w_router,
                                               row(router_bias))

    dest, blocks, zero_start, zero_cnt = _expert_plan(
        top_e[:, :TOP_K], rank[:, :TOP_K], counts, t)
    xs, shared = _dispatch_shared(h1, dest, zero_start, zero_cnt, blocks[0].shape[0],
                                  shared_gate.astype(BF16), shared_up.astype(BF16),
                                  shared_down.astype(BF16))
    ys = _routed_experts(xs, blocks, exp_gate, exp_up, exp_down)
    out = _final(ys, dest, h1, shared, top_w, row(ln2_w), row(ln2_b))
    return out.reshape(batch, seq, d)


def kernel(x, positions, w_in, w_ret_proj, w_diff_proj, w_out, lambda_q1, lambda_k1, lambda_q2,
           lambda_k2, diff_norm_w, ln1_w, ln1_b, w_router, router_bias, exp_gate, exp_up,
           exp_down, shared_gate, shared_up, shared_down, ln2_w, ln2_b):
    h = x
    for l in range(w_in.shape[0]):
        lambda_init = 0.8 - 0.6 * math.exp(-0.3 * l)
        h = _layer(h, positions, w_in[l], w_ret_proj[l], w_diff_proj[l], w_out[l],
                   lambda_q1[l], lambda_k1[l], lambda_q2[l], lambda_k2[l], diff_norm_w[l],
                   ln1_w[l], ln1_b[l], w_router[l], router_bias[l], exp_gate[l], exp_up[l],
                   exp_down[l], shared_gate[l], shared_up[l], shared_down[l],
                   ln2_w[l], ln2_b[l], lambda_init)
    return h
```

```python
import functools
import math

import jax
import jax.numpy as jnp
from jax import lax
from jax.experimental import pallas as pl
from jax.experimental.pallas import tpu as pltpu

F32 = jnp.float32
BF16 = jnp.bfloat16

D_MODEL = 4096
CHUNK = 64
RET_HEADS = 8
RET_QK_DIM = 256
RET_V_DIM = 512
RET_THETA = 10000.0
DIFF_HEADS = 16
DIFF_QK_DIM = 128
DIFF_V_DIM = 256
ROPE_THETA = 10000.0
RET_QW = RET_HEADS * RET_QK_DIM
RET_VW = RET_HEADS * RET_V_DIM
DIFF_QW = DIFF_HEADS * 2 * DIFF_QK_DIM
DIFF_VW = DIFF_HEADS * DIFF_V_DIM
N_EXPERTS = 64
TOP_K = 8
N_GROUPS = 8
TOPK_GROUPS = 4
EXPERT_DIM = 512
ROUTED_SCALE = 2.5
DEPTH = 1
DN_ALPHA = (2.0 * DEPTH) ** 0.25
EPS = 1e-5
LOG2_E = math.log2(math.e)

LANES = 128
SUBLANES = 8
MIB = 1024 * 1024
VMEM_LIMIT = 56 * MIB
EXPERT_VMEM_LIMIT = 60 * MIB

PROJ_TM, PROJ_TN = 1024, 512
MERGE_TM, MERGE_TN = 512, 512
RET_BLOCK = 256
RET_HEADS_PER_STEP = 2
ATT_BLOCK = 512
ATT_KEY_WIDTH = 1024
LN_TM = 256
MOE_BLOCK = 256
DISPATCH_TM = 128
FINAL_TM = 128


def _params(n_axes, vmem_limit=VMEM_LIMIT):
    return pltpu.CompilerParams(dimension_semantics=("arbitrary",) * n_axes,
                                vmem_limit_bytes=vmem_limit)


def _sigmoid(v):
    return 1.0 / (1.0 + jnp.exp(-v))


def _proj_kernel(*refs, mode, scale, tn):
    if mode in ("ret_rot", "diff_rot"):
        x_ref, w_ref, c_ref, s_ref, o_ref = refs
    else:
        x_ref, w_ref, o_ref = refs
    acc = jnp.dot(x_ref[...], w_ref[...].astype(BF16), preferred_element_type=F32)
    if mode == "ret_rot":
        cos, sin = c_ref[...], s_ref[...]
        for h in range(tn // RET_QK_DIM):
            lo = h * RET_QK_DIM
            x1 = acc[:, lo:lo + LANES]
            x2 = acc[:, lo + LANES:lo + 2 * LANES]
            o_ref[:, lo:lo + LANES] = ((x1 * cos - x2 * sin) * scale).astype(o_ref.dtype)
            o_ref[:, lo + LANES:lo + 2 * LANES] = ((x2 * cos + x1 * sin) * scale).astype(o_ref.dtype)
    elif mode == "diff_rot":
        c, s = c_ref[...], s_ref[...]
        for g in range(tn // LANES):
            xg = acc[:, g * LANES:(g + 1) * LANES]
            rot = pltpu.roll(xg, LANES // 2, axis=1)
            o_ref[:, g * LANES:(g + 1) * LANES] = ((xg * c + rot * s) * scale).astype(o_ref.dtype)
    elif mode == "silu":
        o_ref[...] = (acc * _sigmoid(acc)).astype(o_ref.dtype)
    elif mode == "sigmoid":
        o_ref[...] = _sigmoid(acc).astype(o_ref.dtype)
    else:
        o_ref[...] = acc.astype(o_ref.dtype)


def _proj(xb, w, col_off, width, mode, scale=1.0, rot=None):
    t, d = xb.shape
    tm, tn = min(PROJ_TM, t), PROJ_TN
    joff = col_off // tn
    in_specs = [pl.BlockSpec((tm, d), lambda i, j: (i, 0)),
                pl.BlockSpec((d, tn), lambda i, j: (0, j + joff))]
    args = [xb, w]
    if rot is not None:
        in_specs += [pl.BlockSpec((tm, LANES), lambda i, j: (i, 0))] * 2
        args += list(rot)
    return pl.pallas_call(
        functools.partial(_proj_kernel, mode=mode, scale=scale, tn=tn),
        grid=(t // tm, width // tn),
        in_specs=in_specs,
        out_specs=pl.BlockSpec((tm, tn), lambda i, j: (i, j)),
        out_shape=jax.ShapeDtypeStruct((t, width), BF16),
        compiler_params=_params(2),
        name="proj_" + mode,
    )(*args)


def _ret_kernel(q_ref, k_ref, v_ref, g_ref, dm_ref, qd_ref, kd_ref, cd_ref, o_ref, r_ref):
    @pl.when(pl.program_id(2) == 0)
    def _():
        r_ref[...] = jnp.zeros_like(r_ref)

    for hh in range(RET_HEADS_PER_STEP):
        qk = slice(hh * RET_QK_DIM, (hh + 1) * RET_QK_DIM)
        vv = slice(hh * RET_V_DIM, (hh + 1) * RET_V_DIM)
        q, k, v = q_ref[:, qk], k_ref[:, qk], v_ref[:, vv]
        s = lax.dot_general(q, k, (((1,), (1,)), ((), ())), preferred_element_type=F32)
        s = (s * dm_ref[hh]).astype(BF16)
        inner = jnp.dot(s, v, preferred_element_type=F32)
        qs = (q.astype(F32) * qd_ref[hh]).astype(BF16)
        r = r_ref[hh]
        cross = jnp.dot(qs, r.astype(BF16), preferred_element_type=F32)
        ks = (k.astype(F32) * kd_ref[hh]).astype(BF16)
        r_ref[hh] = r * cd_ref[hh] + lax.dot_general(
            ks, v, (((0,), (0,)), ((), ())), preferred_element_type=F32)
        y = inner + cross
        mu = jnp.mean(y, axis=-1, keepdims=True)
        yc = y - mu
        var = jnp.mean(yc * yc, axis=-1, keepdims=True)
        o_ref[:, vv] = (yc * lax.rsqrt(var + EPS) * g_ref[:, vv].astype(F32)).astype(o_ref.dtype)


def _retention_tables(blk):
    log_g = jnp.log1p(-jnp.exp2(-5.0 - jnp.arange(RET_HEADS, dtype=F32)))
    idx = jnp.arange(blk, dtype=F32)
    chunk = jnp.arange(blk) // CHUNK
    visible = chunk[None, :] <= chunk[:, None]
    dm = jnp.where(visible[None],
                   jnp.exp(log_g[:, None, None] * jnp.abs(idx[:, None] - idx[None, :])), 0.0)
    qd = jnp.exp(log_g[:, None] * (idx + 1.0))
    kd = jnp.exp(log_g[:, None] * (blk - 1.0 - idx))
    cd = jnp.exp(log_g * blk)
    qd = jnp.broadcast_to(qd[:, :, None], (RET_HEADS, blk, RET_QK_DIM))
    kd = jnp.broadcast_to(kd[:, :, None], (RET_HEADS, blk, RET_QK_DIM))
    cd = jnp.broadcast_to(cd[:, None, None], (RET_HEADS, 1, RET_V_DIM))
    return dm, qd, kd, cd


def _retention(rq, rk, rv, rg, batch, seq):
    t = batch * seq
    blk = min(RET_BLOCK, seq)
    nl = seq // blk
    dm, qd, kd, cd = _retention_tables(blk)
    hs = RET_HEADS_PER_STEP
    row = lambda b, h, l: (b * nl + l, h)
    head = lambda b, h, l: (h, 0, 0)
    return pl.pallas_call(
        _ret_kernel,
        grid=(batch, RET_HEADS // hs, nl),
        in_specs=[pl.BlockSpec((blk, hs * RET_QK_DIM), row),
                  pl.BlockSpec((blk, hs * RET_QK_DIM), row),
                  pl.BlockSpec((blk, hs * RET_V_DIM), row),
                  pl.BlockSpec((blk, hs * RET_V_DIM), row),
                  pl.BlockSpec((hs, blk, blk), head),
                  pl.BlockSpec((hs, blk, RET_QK_DIM), head),
                  pl.BlockSpec((hs, blk, RET_QK_DIM), head),
                  pl.BlockSpec((hs, 1, RET_V_DIM), head)],
        out_specs=pl.BlockSpec((blk, hs * RET_V_DIM), row),
        out_shape=jax.ShapeDtypeStruct((t, RET_VW), BF16),
        scratch_shapes=[pltpu.VMEM((hs, RET_QK_DIM, RET_V_DIM), F32)],
        compiler_params=_params(3),
        name="retention",
    )(rq, rk, rv, rg, dm, qd, kd, cd)


def _attn_kernel(lq1_ref, lk1_ref, lq2_ref, lk2_ref, nw_ref, q_ref, k_ref, v_ref, o_ref,
                 m_ref, l_ref, acc_ref, *, blk, lambda_init):
    qi = pl.program_id(2)
    m_ref[...] = jnp.full_like(m_ref, -jnp.inf)
    l_ref[...] = jnp.zeros_like(l_ref)
    acc_ref[...] = jnp.zeros_like(acc_ref)
    q = q_ref[...]

    def block(first_key, width, row0=0, rows=blk, diag_key0=None):
        off = pl.multiple_of(first_key, width)
        kb = k_ref[pl.ds(off, width), :]
        vb = v_ref[pl.ds(off, width), :]
        rs = slice(row0, row0 + rows)
        if diag_key0 is not None:
            rchunk = (row0 + lax.broadcasted_iota(jnp.int32, (rows, width), 0)) // CHUNK
            cchunk = (diag_key0 + lax.broadcasted_iota(jnp.int32, (rows, width), 1)) // CHUNK
            visible = cchunk <= rchunk
        for c in range(2):
            qc = q[rs, c * DIFF_QK_DIM:(c + 1) * DIFF_QK_DIM]
            kc = kb[:, c * DIFF_QK_DIM:(c + 1) * DIFF_QK_DIM]
            s = lax.dot_general(qc, kc, (((1,), (1,)), ((), ())), preferred_element_type=F32)
            if diag_key0 is not None:
                s = jnp.where(visible, s, -jnp.inf)
            m_prev = m_ref[c, rs]
            m_new = jnp.maximum(m_prev, jnp.max(s, axis=-1, keepdims=True))
            p = jnp.exp2(s - jnp.tile(m_new, (1, width // LANES)))
            alpha = jnp.exp2(m_prev - m_new)
            l_ref[c, rs] = alpha * l_ref[c, rs] + jnp.sum(p, axis=-1, keepdims=True)
            acc_ref[c, rs] = (jnp.tile(alpha, (1, DIFF_V_DIM // LANES)) * acc_ref[c, rs]
                              + jnp.dot(p.astype(BF16), vb, preferred_element_type=F32))
            m_ref[c, rs] = m_new

    per = max(ATT_KEY_WIDTH // blk, 1)

    def body(j, carry):
        block(j * (per * blk), per * blk)
        return carry

    lax.fori_loop(0, qi // per, body, 0)

    if per == 2:
        @pl.when(lax.rem(qi, 2) == 1)
        def _():
            block((qi - 1) * blk, blk)

    half = blk // 2
    block(qi * blk, half, diag_key0=0)
    block(qi * blk + half, half, row0=half, rows=half, diag_key0=half)

    lam = (jnp.exp(jnp.sum(lq1_ref[...] * lk1_ref[...], axis=-1, keepdims=True))
           - jnp.exp(jnp.sum(lq2_ref[...] * lk2_ref[...], axis=-1, keepdims=True))
           + lambda_init)
    rep = DIFF_V_DIM // LANES
    o = (acc_ref[0] / jnp.tile(l_ref[0], (1, rep))
         - lam * (acc_ref[1] / jnp.tile(l_ref[1], (1, rep))))
    ms = jnp.mean(o * o, axis=-1, keepdims=True)
    o_ref[...] = (o * lax.rsqrt(ms + EPS) * nw_ref[...] * (1.0 - lambda_init)).astype(o_ref.dtype)


def _diff_attention(dq, dk, dv, lq1, lk1, lq2, lk2, norm_w, batch, seq, lambda_init):
    t = batch * seq
    blk = min(ATT_BLOCK, seq)
    nq = seq // blk
    hw = 2 * DIFF_QK_DIM
    vec = lambda b, h, i: (0, 0)
    return pl.pallas_call(
        functools.partial(_attn_kernel, blk=blk, lambda_init=lambda_init),
        grid=(batch, DIFF_HEADS, nq),
        in_specs=[pl.BlockSpec((1, DIFF_QK_DIM), vec)] * 4
        + [pl.BlockSpec((1, DIFF_V_DIM), vec),
           pl.BlockSpec((blk, hw), lambda b, h, i: (b * nq + i, h)),
           pl.BlockSpec((seq, hw), lambda b, h, i: (b, h)),
           pl.BlockSpec((seq, DIFF_V_DIM), lambda b, h, i: (b, h))],
        out_specs=pl.BlockSpec((blk, DIFF_V_DIM), lambda b, h, i: (b * nq + i, h)),
        out_shape=jax.ShapeDtypeStruct((t, DIFF_VW), BF16),
        scratch_shapes=[pltpu.VMEM((2, blk, LANES), F32),
                        pltpu.VMEM((2, blk, LANES), F32),
                        pltpu.VMEM((2, blk, DIFF_V_DIM), F32)],
        compiler_params=_params(3),
        name="diff_attention",
    )(lq1, lk1, lq2, lk2, norm_w, dq, dk, dv)


def _merge_kernel(r_ref, d_ref, wr_ref, wd_ref, gr_ref, gd_ref, o_ref):
    a = jnp.dot(r_ref[...], wr_ref[...], preferred_element_type=F32)
    b = jnp.dot(d_ref[...], wd_ref[...], preferred_element_type=F32)
    o_ref[...] = (gr_ref[...].astype(F32) * a + gd_ref[...].astype(F32) * b).astype(o_ref.dtype)


def _merge(ret, da, wr, wd, gr, gd):
    t, kdim = ret.shape
    n = wr.shape[1]
    tm, tn = min(MERGE_TM, t), MERGE_TN
    lhs = pl.BlockSpec((tm, kdim), lambda i, j: (i, 0))
    rhs = pl.BlockSpec((kdim, tn), lambda i, j: (0, j))
    tile = pl.BlockSpec((tm, tn), lambda i, j: (i, j))
    return pl.pallas_call(
        _merge_kernel,
        grid=(t // tm, n // tn),
        in_specs=[lhs, lhs, rhs, rhs, tile, tile],
        out_specs=tile,
        out_shape=jax.ShapeDtypeStruct((t, n), BF16),
        compiler_params=_params(2),
        name="merge",
    )(ret, da, wr, wd, gr, gd)


def _outproj_kernel(m_ref, w_ref, x_ref, o_ref):
    o_ref[...] = DN_ALPHA * x_ref[...] + jnp.dot(m_ref[...], w_ref[...].astype(BF16),
                                                  preferred_element_type=F32)


def _outproj(merged, w, x):
    t, kdim = merged.shape
    n = w.shape[1]
    tm, tn = min(PROJ_TM, t), PROJ_TN
    tile = pl.BlockSpec((tm, tn), lambda i, j: (i, j))
    return pl.pallas_call(
        _outproj_kernel,
        grid=(t // tm, n // tn),
        in_specs=[pl.BlockSpec((tm, kdim), lambda i, j: (i, 0)),
                  pl.BlockSpec((kdim, tn), lambda i, j: (0, j)),
                  tile],
        out_specs=tile,
        out_shape=jax.ShapeDtypeStruct((t, n), F32),
        compiler_params=_params(2),
        name="outproj",
    )(merged, w, x)


def _layernorm(y, w, b):
    mu = jnp.mean(y, axis=-1, keepdims=True)
    yc = y - mu
    var = jnp.mean(yc * yc, axis=-1, keepdims=True)
    return yc * lax.rsqrt(var + EPS) * w + b


def _ln_route_kernel(y_ref, lw_ref, lb_ref, wr_ref, rb_ref, h_ref, te_ref, tw_ref, tr_ref, tc_ref,
                     cnt_ref):
    h = _layernorm(y_ref[...], lw_ref[...], lb_ref[...])
    h_ref[...] = h
    hi = h.astype(BF16)
    lo = (h - hi.astype(F32)).astype(BF16)
    w = wr_ref[...]
    whi = w.astype(BF16)
    wlo = (w - whi.astype(F32)).astype(BF16)
    logits = (jnp.dot(hi, whi, preferred_element_type=F32)
              + (jnp.dot(hi, wlo, preferred_element_type=F32)
                 + jnp.dot(lo, whi, preferred_element_type=F32)))
    scores = _sigmoid(logits)
    biased = scores + rb_ref[...]
    tm = scores.shape[0]
    neg = -jnp.inf
    lane_i = lax.broadcasted_iota(jnp.int32, (tm, N_EXPERTS), 1)
    lane = lane_i.astype(F32)
    per_group = N_EXPERTS // N_GROUPS
    grp = lane_i // per_group

    gscores = []
    gs_lane = jnp.zeros((tm, N_EXPERTS), F32)
    for g in range(N_GROUPS):
        mk = grp == g
        vals = jnp.where(mk, biased, neg)
        m1 = jnp.max(vals, axis=-1, keepdims=True)
        i1 = jnp.min(jnp.where(vals == m1, lane, float(N_EXPERTS)), axis=-1, keepdims=True)
        m2 = jnp.max(jnp.where(lane == i1, neg, vals), axis=-1, keepdims=True)
        gscores.append(m1 + m2)
        gs_lane = jnp.where(mk, m1 + m2, gs_lane)
    beaten = jnp.zeros((tm, N_EXPERTS), F32)
    for g in range(N_GROUPS):
        wins = jnp.where(gscores[g] > gs_lane, 1.0,
                         jnp.where(gscores[g] == gs_lane,
                                   jnp.where(grp > g, 1.0, 0.0), 0.0))
        beaten = beaten + wins
    cur = jnp.where(beaten < float(TOPK_GROUPS), biased, neg)

    out_lane = lax.broadcasted_iota(jnp.int32, (tm, LANES), 1)
    sel_e = jnp.zeros((tm, LANES), F32)
    sel_w = jnp.zeros((tm, LANES), F32)
    wsum = jnp.zeros((tm, 1), F32)
    hits = []
    for k in range(TOP_K):
        m = jnp.max(cur, axis=-1, keepdims=True)
        idx = jnp.min(jnp.where(cur == m, lane, float(N_EXPERTS)), axis=-1, keepdims=True)
        hit = lane == idx
        wk = jnp.sum(jnp.where(hit, scores, 0.0), axis=-1, keepdims=True)
        cur = jnp.where(hit, neg, cur)
        wsum = wsum + wk
        sel_e = jnp.where(out_lane == k, idx, sel_e)
        sel_w = jnp.where(out_lane == k, wk, sel_w)
        hits.append(hit)
    te_ref[...] = sel_e.astype(jnp.int32)
    tw_ref[...] = sel_w / wsum * ROUTED_SCALE

    @pl.when(pl.program_id(0) == 0)
    def _():
        cnt_ref[...] = jnp.zeros_like(cnt_ref)

    chosen = jnp.zeros((tm, N_EXPERTS), F32)
    for hit in hits:
        chosen = chosen + jnp.where(hit, 1.0, 0.0)
    earlier = (lax.broadcasted_iota(jnp.int32, (tm, tm), 0)
               > lax.broadcasted_iota(jnp.int32, (tm, tm), 1))
    prefix = jnp.dot(jnp.where(earlier, 1.0, 0.0).astype(BF16), chosen.astype(BF16),
                     preferred_element_type=F32) + cnt_ref[...]
    sel_r = jnp.zeros((tm, LANES), F32)
    for k, hit in enumerate(hits):
        rk = jnp.sum(jnp.where(hit, prefix, 0.0), axis=-1, keepdims=True)
        sel_r = jnp.where(out_lane == k, rk, sel_r)
    tr_ref[...] = sel_r.astype(jnp.int32)
    cnt_ref[...] = cnt_ref[...] + jnp.sum(chosen, axis=0, keepdims=True)
    tc_ref[...] = cnt_ref[...]


def _ln_route(y, ln_w, ln_b, w_router, router_bias):
    t, d = y.shape
    tm = min(LN_TM, t)
    rowt = pl.BlockSpec((tm, d), lambda i: (i, 0))
    vec = pl.BlockSpec((1, d), lambda i: (0, 0))
    narrow = pl.BlockSpec((tm, LANES), lambda i: (i, 0))
    return pl.pallas_call(
        _ln_route_kernel,
        grid=(t // tm,),
        in_specs=[rowt, vec, vec,
                  pl.BlockSpec((d, N_EXPERTS), lambda i: (0, 0)),
                  pl.BlockSpec((1, N_EXPERTS), lambda i: (0, 0))],
        out_specs=[rowt, narrow, narrow, narrow,
                   pl.BlockSpec((1, N_EXPERTS), lambda i: (0, 0))],
        out_shape=[jax.ShapeDtypeStruct((t, d), F32),
                   jax.ShapeDtypeStruct((t, LANES), jnp.int32),
                   jax.ShapeDtypeStruct((t, LANES), F32),
                   jax.ShapeDtypeStruct((t, LANES), jnp.int32),
                   jax.ShapeDtypeStruct((1, N_EXPERTS), F32)],
        scratch_shapes=[pltpu.VMEM((1, N_EXPERTS), F32)],
        compiler_params=_params(1),
        name="ln_route",
    )(y, ln_w, ln_b, w_router, router_bias)


def _expert_plan(top_e, rank, counts, t):
    nb = t * TOP_K // MOE_BLOCK + N_EXPERTS
    counts = counts.reshape(N_EXPERTS).astype(jnp.int32)
    padded = (counts + MOE_BLOCK - 1) // MOE_BLOCK * MOE_BLOCK
    pad_end = jnp.cumsum(padded)
    pad_start = pad_end - padded
    experts = jnp.arange(N_EXPERTS, dtype=jnp.int32)
    dest = jnp.sum(jnp.where(top_e[..., None] == experts, pad_start, 0), axis=-1) + rank
    n_active = (pad_end[-1] // MOE_BLOCK).astype(jnp.int32)
    blk_start = jnp.arange(nb, dtype=jnp.int32) * MOE_BLOCK
    block_e = jnp.sum((pad_end[None, :] <= blk_start[:, None]).astype(jnp.int32), axis=-1)
    block_e = jnp.minimum(block_e, N_EXPERTS - 1)
    last_e = jnp.sum((pad_end <= (n_active - 1) * MOE_BLOCK).astype(jnp.int32))
    block_e = jnp.where(jnp.arange(nb) < n_active, block_e, jnp.minimum(last_e, N_EXPERTS - 1))
    first = jnp.concatenate([jnp.ones((1,), jnp.int32),
                             (block_e[1:] != block_e[:-1]).astype(jnp.int32)])
    later = (experts[None, :] > experts[:, None]) & (counts[None, :] > 0)
    next_e = jnp.min(jnp.where(later, experts[None, :], N_EXPERTS), axis=-1)
    next_e = jnp.where(next_e == N_EXPERTS, experts, next_e)
    block_next = jnp.sum(jnp.where(block_e[:, None] == experts[None, :], next_e[None, :], 0),
                         axis=-1)
    tail_pieces = (nb * MOE_BLOCK - pad_end[-1]) // (MOE_BLOCK // 2)
    zero_start = jnp.concatenate([pad_start + counts, pad_end[-1:]]).astype(jnp.int32)
    zero_cnt = jnp.concatenate([padded - counts, tail_pieces[None]]).astype(jnp.int32)
    blocks = (block_e.astype(jnp.int32), block_next.astype(jnp.int32), first,
              n_active.reshape(1))
    return dest.astype(jnp.int32), blocks, zero_start, zero_cnt


def _halves(w):
    half = w.shape[-1] // 2
    return w[:, :half], w[:, half:]


def _dispatch_kernel(zs_ref, zn_ref, dest_ref, h_ref, wg_ref, wu_ref, wd_ref, xs_hbm, s_ref,
                     buf, zbuf, sem, zsem, *, tm):
    i = pl.program_id(0)
    n = pl.num_programs(0)
    slot = lax.rem(i, 2)

    def zero_rows(wait):
        def zero_copy(off, rows):
            cp = pltpu.make_async_copy(zbuf.at[pl.ds(0, rows)], xs_hbm.at[pl.ds(off, rows)], zsem)
            if wait:
                cp.wait()
            else:
                cp.start()

        def single_rows(off, count):
            def one(j, carry):
                zero_copy(off + j, 1)
                return carry
            lax.fori_loop(0, count, one, 0)

        def body(e, carry):
            cnt = zn_ref[e]
            start = zs_ref[e]
            head = jnp.minimum((-start) & (SUBLANES - 1), cnt)
            single_rows(start, head)
            rest = cnt - head
            off = start + head
            rows = MOE_BLOCK // 2
            while rows >= SUBLANES:
                @pl.when((rest & rows) != 0)
                def _():
                    zero_copy(pl.multiple_of(off, SUBLANES), rows)
                off = off + (rest & rows)
                rows //= 2
            single_rows(off, rest & (SUBLANES - 1))
            return carry
        lax.fori_loop(0, N_EXPERTS, body, 0)

        def tail(c, carry):
            zero_copy(pl.multiple_of(zs_ref[N_EXPERTS] + c * (MOE_BLOCK // 2), MOE_BLOCK // 2),
                      MOE_BLOCK // 2)
            return carry
        lax.fori_loop(0, zn_ref[N_EXPERTS], tail, 0)

    @pl.when(i == 0)
    def _():
        zbuf[...] = jnp.zeros_like(zbuf)
        zero_rows(False)
        zero_rows(True)

    def wait_rows(s):
        for _ in range(TOP_K):
            pltpu.make_async_copy(buf.at[s], xs_hbm.at[pl.ds(0, tm)], sem.at[s]).wait()

    @pl.when(i >= 2)
    def _():
        wait_rows(slot)

    h = h_ref[...]
    buf[slot] = h
    for r in range(tm):
        for k in range(TOP_K):
            pltpu.make_async_copy(buf.at[slot, pl.ds(r, 1)],
                                  xs_hbm.at[pl.ds(dest_ref[0, 0, r * TOP_K + k], 1)],
                                  sem.at[slot]).start()

    x = h.astype(BF16)
    g = jnp.dot(x, wg_ref[...], preferred_element_type=F32)
    u = jnp.dot(x, wu_ref[...], preferred_element_type=F32)
    act = (g * _sigmoid(g) * u).astype(BF16)
    s_ref[...] = jnp.dot(act, wd_ref[...], preferred_element_type=F32)

    @pl.when(i == n - 1)
    def _():
        wait_rows(slot)

    @pl.when((i == n - 1) & (n >= 2))
    def _():
        wait_rows(1 - slot)


def _dispatch_shared(h, dest, zero_start, zero_cnt, nb, wg, wu, wd):
    t, d = h.shape
    e_dim = wg.shape[-1]
    tm = min(DISPATCH_TM, t)
    rowt = pl.BlockSpec((tm, d), lambda i, zs, zn: (i, 0))
    whole = lambda shape: pl.BlockSpec(shape, lambda i, zs, zn: (0, 0))
    grid_spec = pltpu.PrefetchScalarGridSpec(
        num_scalar_prefetch=2,
        grid=(t // tm,),
        in_specs=[pl.BlockSpec((1, 1, tm * TOP_K), lambda i, zs, zn: (i, 0, 0),
                               memory_space=pltpu.SMEM),
                  rowt, whole((d, e_dim)), whole((d, e_dim)), whole((e_dim, d))],
        out_specs=[pl.BlockSpec(memory_space=pl.ANY), rowt],
        scratch_shapes=[pltpu.VMEM((2, tm, d), F32),
                        pltpu.VMEM((MOE_BLOCK // 2, d), F32),
                        pltpu.SemaphoreType.DMA((2,)),
                        pltpu.SemaphoreType.DMA(())],
    )
    return pl.pallas_call(
        functools.partial(_dispatch_kernel, tm=tm),
        grid_spec=grid_spec,
        out_shape=[jax.ShapeDtypeStruct((nb * MOE_BLOCK, d), F32),
                   jax.ShapeDtypeStruct((t, d), F32)],
        compiler_params=_params(1),
        name="dispatch_shared",
    )(zero_start, zero_cnt, dest.reshape(t // tm, 1, tm * TOP_K), h, wg, wu, wd)


def _expert_kernel(be_ref, nx_ref, first_ref, na_ref, x_ref, wg_hbm, wu_hbm, wd_hbm, o_ref,
                   sg, su, sd, wg_ref, wu_ref, wd_ref, sem):
    i = pl.program_id(0)
    staged = ((wg_hbm, sg, wg_ref), (wu_hbm, su, wu_ref), (wd_hbm, sd, wd_ref))

    def fetch(e):
        for s, (src, stage, _) in enumerate(staged):
            pltpu.make_async_copy(src.at[e], stage, sem.at[s]).start()

    @pl.when(i < na_ref[0])
    def _():
        @pl.when(i == 0)
        def _():
            fetch(be_ref[0])

        def compute(first):
            def working(s):
                _, stage, work = staged[s]
                if first:
                    work[...] = stage[...].astype(BF16)
                return work

            if first:
                for s, (src, stage, _) in enumerate(staged):
                    pltpu.make_async_copy(src.at[0], stage, sem.at[s]).wait()
            lo, hi = _halves(x_ref[...])
            lo, hi = lo.astype(BF16), hi.astype(BF16)
            half = lo.shape[-1]

            def in_proj(w_ref):
                return (jnp.dot(lo, w_ref[:half, :], preferred_element_type=F32)
                        + jnp.dot(hi, w_ref[half:, :], preferred_element_type=F32))

            g = in_proj(working(0))
            u = in_proj(working(1))
            act = (g * _sigmoid(g) * u).astype(BF16)
            o_ref[...] = jnp.dot(act, working(2)[...], preferred_element_type=F32)

        @pl.when(first_ref[i] == 1)
        def _():
            compute(True)

            @pl.when(nx_ref[i] != be_ref[i])
            def _():
                fetch(nx_ref[i])

        @pl.when(first_ref[i] != 1)
        def _():
            compute(False)

    @pl.when(i >= na_ref[0])
    def _():
        o_ref[...] = jnp.zeros_like(o_ref)


def _routed_experts(xs, blocks, wg, wu, wd):
    p, dh = xs.shape
    nb = p // MOE_BLOCK
    d, e_dim = wg.shape[-2:]
    anywhere = pl.BlockSpec(memory_space=pl.ANY)
    grid_spec = pltpu.PrefetchScalarGridSpec(
        num_scalar_prefetch=4,
        grid=(nb,),
        in_specs=[pl.BlockSpec((MOE_BLOCK, dh),
                               lambda i, be, nx, ff, na:
                               (jnp.minimum(i, jnp.maximum(na[0] - 1, 0)), 0)),
                  anywhere, anywhere, anywhere],
        out_specs=pl.BlockSpec((MOE_BLOCK, dh), lambda i, be, nx, ff, na: (i, 0)),
        scratch_shapes=[pltpu.VMEM((d, e_dim), F32), pltpu.VMEM((d, e_dim), F32),
                        pltpu.VMEM((e_dim, d), F32),
                        pltpu.VMEM((d, e_dim), BF16), pltpu.VMEM((d, e_dim), BF16),
                        pltpu.VMEM((e_dim, d), BF16),
                        pltpu.SemaphoreType.DMA((3,))],
    )
    return pl.pallas_call(
        _expert_kernel,
        grid_spec=grid_spec,
        out_shape=jax.ShapeDtypeStruct((p, dh), F32),
        compiler_params=_params(1, EXPERT_VMEM_LIMIT),
        name="routed_experts",
    )(*blocks, xs, wg, wu, wd)


def _final_kernel(dest_ref, destn_ref, ys_hbm, h_ref, s_ref, tw_ref, lw_ref, lb_ref, o_ref,
                  gbuf, sem, *, tm):
    i = pl.program_id(0)
    n = pl.num_programs(0)
    slot = lax.rem(i, 2)

    def row_copy(idx_ref, s, r, k):
        return pltpu.make_async_copy(ys_hbm.at[pl.ds(idx_ref[0, 0, r * TOP_K + k], 1)],
                                     gbuf.at[s, k, pl.ds(r, 1)], sem.at[s])

    def gather_wait(s):
        for k in range(TOP_K):
            pltpu.make_async_copy(ys_hbm.at[pl.ds(0, tm)], gbuf.at[s, k], sem.at[s]).wait()

    @pl.when(i == 0)
    def _():
        def body(r, carry):
            for k in range(TOP_K):
                row_copy(dest_ref, 0, r, k).start()
            return carry
        lax.fori_loop(0, tm, body, 0)

    gather_wait(slot)
    for r in range(tm):
        for k in range(TOP_K):
            row_copy(destn_ref, 1 - slot, r, k).start()

    tw = tw_ref[...]
    half = gbuf.shape[-1] // 2
    y_lo = DN_ALPHA * h_ref[:, :half] + s_ref[:, :half]
    y_hi = DN_ALPHA * h_ref[:, half:] + s_ref[:, half:]
    for k in range(TOP_K):
        lo, hi = _halves(gbuf[slot, k])
        y_lo = y_lo + tw[:, k:k + 1] * lo
        y_hi = y_hi + tw[:, k:k + 1] * hi
    inv_d = 1.0 / (2 * half)
    mu = (jnp.sum(y_lo, axis=-1, keepdims=True) + jnp.sum(y_hi, axis=-1, keepdims=True)) * inv_d
    c_lo, c_hi = y_lo - mu, y_hi - mu
    var = (jnp.sum(c_lo * c_lo, axis=-1, keepdims=True)
           + jnp.sum(c_hi * c_hi, axis=-1, keepdims=True)) * inv_d
    r = lax.rsqrt(var + EPS)
    o_ref[:, :half] = c_lo * r * lw_ref[:, :half] + lb_ref[:, :half]
    o_ref[:, half:] = c_hi * r * lw_ref[:, half:] + lb_ref[:, half:]

    @pl.when(i == n - 1)
    def _():
        gather_wait(1 - slot)


def _final(ys, dest, h, shared, top_w, ln_w, ln_b):
    t, d = h.shape
    tm = min(FINAL_TM, t)
    n = t // tm
    idx = dest.reshape(n, 1, tm * TOP_K)
    idx_spec = lambda fn: pl.BlockSpec((1, 1, tm * TOP_K), fn, memory_space=pltpu.SMEM)
    rowt = pl.BlockSpec((tm, d), lambda i: (i, 0))
    vec = pl.BlockSpec((1, d), lambda i: (0, 0))
    return pl.pallas_call(
        functools.partial(_final_kernel, tm=tm),
        grid=(n,),
        in_specs=[idx_spec(lambda i: (i, 0, 0)),
                  idx_spec(lambda i: (jnp.minimum(i + 1, n - 1), 0, 0)),
                  pl.BlockSpec(memory_space=pl.ANY),
                  rowt, rowt,
                  pl.BlockSpec((tm, LANES), lambda i: (i, 0)),
                  vec, vec],
        out_specs=rowt,
        out_shape=jax.ShapeDtypeStruct((t, d), F32),
        scratch_shapes=[pltpu.VMEM((2, TOP_K, tm, d), F32),
                        pltpu.SemaphoreType.DMA((2,))],
        compiler_params=_params(1),
        name="combine_ln2",
    )(idx, idx, ys, h, shared, top_w, ln_w, ln_b)


def _rotary_tables(positions):
    pos = positions.reshape(-1).astype(F32)[:, None]
    ret_freqs = 1.0 / (RET_THETA ** jnp.linspace(0.0, 1.0, RET_QK_DIM // 2, dtype=F32))
    ang = pos * ret_freqs
    ret_rot = (jnp.cos(ang), jnp.sin(ang))
    rope_freqs = ROPE_THETA ** (-jnp.arange(0, DIFF_QK_DIM, 2, dtype=F32) / DIFF_QK_DIM)
    ang = pos * rope_freqs
    cos, sin = jnp.cos(ang), jnp.sin(ang)
    diff_rot = (jnp.concatenate([cos, cos], axis=-1), jnp.concatenate([-sin, sin], axis=-1))
    return ret_rot, diff_rot


def _layer(h, positions, w_in, w_ret_proj, w_diff_proj, w_out, lq1, lk1, lq2, lk2, diff_norm_w,
           ln1_w, ln1_b, w_router, router_bias, exp_gate, exp_up, exp_down,
           shared_gate, shared_up, shared_down, ln2_w, ln2_b, lambda_init):
    batch, seq, d = h.shape
    t = batch * seq
    x = h.reshape(t, d)
    xb = x.astype(BF16)
    ret_rot, diff_rot = _rotary_tables(positions)

    off = 0
    rq = _proj(xb, w_in, off, RET_QW, "ret_rot", 1.0, ret_rot); off += RET_QW
    rk = _proj(xb, w_in, off, RET_QW, "ret_rot", RET_QK_DIM ** -0.5, ret_rot); off += RET_QW
    rv = _proj(xb, w_in, off, RET_VW, "plain"); off += RET_VW
    rg = _proj(xb, w_in, off, RET_VW, "silu"); off += RET_VW
    dq = _proj(xb, w_in, off, DIFF_QW, "diff_rot", LOG2_E * DIFF_QK_DIM ** -0.5, diff_rot)
    off += DIFF_QW
    dk = _proj(xb, w_in, off, DIFF_QW, "diff_rot", 1.0, diff_rot); off += DIFF_QW
    dv = _proj(xb, w_in, off, DIFF_VW, "plain"); off += DIFF_VW
    gate_ret = _proj(xb, w_in, off, d, "sigmoid"); off += d
    gate_diff = _proj(xb, w_in, off, d, "sigmoid")

    ret = _retention(rq, rk, rv, rg, batch, seq)
    row = lambda v: v.reshape(1, -1).astype(F32)
    da = _diff_attention(dq, dk, dv, row(lq1), row(lk1), row(lq2), row(lk2), row(diff_norm_w),
                         batch, seq, lambda_init)
    merged = _merge(ret, da, w_ret_proj.astype(BF16), w_diff_proj.astype(BF16),
                    gate_ret, gate_diff)
    y1 = _outproj(merged, w_out, x)
    h1, top_e, top_w, rank, counts = _ln_route(y1, row(ln1_w), row(ln1_b), w_router,
                                               row(router_bias))

    dest, blocks, zero_start, zero_cnt = _expert_plan(
        top_e[:, :TOP_K], rank[:, :TOP_K], counts, t)
    xs, shared = _dispatch_shared(h1, dest, zero_start, zero_cnt, blocks[0].shape[0],
                                  shared_gate.astype(BF16), shared_up.astype(BF16),
                                  shared_down.astype(BF16))
    ys = _routed_experts(xs, blocks, exp_gate, exp_up, exp_down)
    out = _final(ys, dest, h1, shared, top_w, row(ln2_w), row(ln2_b))
    return out.reshape(batch, seq, d)


def kernel(x, positions, w_in, w_ret_proj, w_diff_proj, w_out, lambda_q1, lambda_k1, lambda_q2,
           lambda_k2, diff_norm_w, ln1_w, ln1_b, w_router, router_bias, exp_gate, exp_up,
           exp_down, shared_gate, shared_up, shared_down, ln2_w, ln2_b):
    h = x
    for l in range(w_in.shape[0]):
        lambda_init = 0.8 - 0.6 * math.exp(-0.3 * l)
        h = _layer(h, positions, w_in[l], w_ret_proj[l], w_diff_proj[l], w_out[l],
                   lambda_q1[l], lambda_k1[l], lambda_q2[l], lambda_k2[l], diff_norm_w[l],
                   ln1_w[l], ln1_b[l], w_router[l], router_bias[l], exp_gate[l], exp_up[l],
                   exp_down[l], shared_gate[l], shared_up[l], shared_down[l],
                   ln2_w[l], ln2_b[l], lambda_init)
    return h
```
